```python
import math
import jax, jax.numpy as jnp
from jax import lax
import numpy as np

D_MODEL = 1024
BATCH = 16
SEQ = 2048
DEPTH = 4

HEAD_DIM = 64
N_BRANCHES = 3
LRU_W = D_MODEL
LRU_BLOCKS = D_MODEL // HEAD_DIM
LRU_BLOCK_W = LRU_W // LRU_BLOCKS
CONV_WIDTH = 4
LRU_C = 8.0
SWA_HEADS = D_MODEL // HEAD_DIM
SWA_KV_HEADS = SWA_HEADS // 4
SWA_GROUP = SWA_HEADS // SWA_KV_HEADS
SWA_WINDOW = 128
DIL_HEADS = D_MODEL // HEAD_DIM
DIL_CONFIGS = ((128, 1), (512, 4), (2048, 16))
FF_HIDDEN = int(math.ceil(8 * D_MODEL / 3 / 256) * 256)
DEEPNORM_ALPHA = (2.0 * DEPTH) ** 0.25
DEEPNORM_BETA = (8.0 * DEPTH) ** -0.25
LN_EPS = 1e-5
NEG_INF = -1e30
IN_WIDTHS = (LRU_W, LRU_W,
             SWA_HEADS * HEAD_DIM, SWA_KV_HEADS * HEAD_DIM, SWA_KV_HEADS * HEAD_DIM,
             DIL_HEADS * HEAD_DIM, DIL_HEADS * HEAD_DIM, DIL_HEADS * HEAD_DIM,
             N_BRANCHES * D_MODEL)
IN_WIDTH = sum(IN_WIDTHS)
BRANCH_W = D_MODEL

kernel_name = "hybrid_rglru_swa_sink_dilated_deepnorm"


def layer_norm(x, g, b):
    xf = x.astype(jnp.float32)
    mu = jnp.mean(xf, axis=-1, keepdims=True)
    var = jnp.mean(jnp.square(xf - mu), axis=-1, keepdims=True)
    y = (xf - mu) * lax.rsqrt(var + LN_EPS)
    return (y * g.astype(jnp.float32) + b.astype(jnp.float32)).astype(x.dtype)


def banded_window_attention(q, k, v, window, sink=None):
    bsz, L, hk, g, dh = q.shape
    blk = window
    nb = -(-L // blk)
    pad = nb * blk - L
    if pad:
        q = jnp.pad(q, ((0, 0), (0, pad), (0, 0), (0, 0), (0, 0)))
        k = jnp.pad(k, ((0, 0), (0, pad), (0, 0), (0, 0)))
        v = jnp.pad(v, ((0, 0), (0, pad), (0, 0), (0, 0)))
    qb = q.reshape(bsz, nb, blk, hk, g, dh)
    kb = k.reshape(bsz, nb, blk, hk, dh)
    vb = v.reshape(bsz, nb, blk, hk, dh)

    def with_prev(t):
        prev = jnp.pad(t[:, :-1], ((0, 0), (1, 0), (0, 0), (0, 0), (0, 0)))
        return jnp.concatenate([prev, t], axis=2)

    kk = with_prev(kb)
    vv = with_prev(vb)
    s = jnp.einsum('bnqhgd,bnkhd->bhgnqk', qb, kk,
                   preferred_element_type=jnp.float32) * (dh ** -0.5)
    qi = jnp.arange(blk)[:, None]
    kj = jnp.arange(2 * blk)[None, :]
    rel = qi + blk - kj
    key_exists = (jnp.arange(nb)[:, None, None] > 0) | (kj[None] >= blk)
    mask = (rel >= 0)[None] & (rel <= window)[None] & key_exists
    s = jnp.where(mask, s, NEG_INF)
    m = jnp.max(s, axis=-1)
    if sink is not None:
        sink_b = sink.astype(jnp.float32).reshape(hk, g, 1, 1)
        m = jnp.maximum(m, sink_b)
    p = jnp.exp(s - m[..., None])
    denom = jnp.sum(p, axis=-1)
    if sink is not None:
        denom = denom + jnp.exp(sink_b - m)
    o = jnp.einsum('bhgnqk,bnkhd->bnqhgd', p.astype(vv.dtype), vv,
                   preferred_element_type=jnp.float32)
    den_t = jnp.transpose(denom, (0, 3, 4, 1, 2))
    lse_t = jnp.transpose(m + jnp.log(denom), (0, 3, 4, 1, 2))
    o = (o / den_t[..., None]).reshape(bsz, nb * blk, hk, g, dh)[:, :L]
    lse = lse_t.reshape(bsz, nb * blk, hk, g)[:, :L]
    return o.astype(q.dtype), lse


def dilated_attention(q, k, v):
    bsz, S, H, dh = q.shape
    outs, lses = [], []
    for window, dil in DIL_CONFIGS:
        Ls = S // dil

        def to_sub(t):
            return t.reshape(bsz, Ls, dil, H, dh).transpose(0, 2, 1, 3, 4).reshape(bsz * dil, Ls, H, dh)

        o, lse = banded_window_attention(to_sub(q)[:, :, :, None], to_sub(k), to_sub(v), window // dil)
        o = o[:, :, :, 0].reshape(bsz, dil, Ls, H, dh).transpose(0, 2, 1, 3, 4).reshape(bsz, S, H, dh)
        lse = lse[:, :, :, 0].reshape(bsz, dil, Ls, H).transpose(0, 2, 1, 3).reshape(bsz, S, H)
        outs.append(o)
        lses.append(lse)
    w = jax.nn.softmax(jnp.stack(lses, axis=0), axis=0)
    o = jnp.sum(w[..., None] * jnp.stack(outs, axis=0).astype(jnp.float32), axis=0)
    return o.astype(q.dtype)


def rglru_branch(xr, gate_in, conv_w, conv_b, w_rg, b_rg, w_ig, b_ig, lru_lambda):
    bsz, S, W = xr.shape
    xp = jnp.pad(xr, ((0, 0), (CONV_WIDTH - 1, 0), (0, 0)))
    xc = sum(xp[:, j:j + S] * conv_w[j] for j in range(CONV_WIDTH)) + conv_b
    xh = xc.reshape(bsz, S, LRU_BLOCKS, LRU_BLOCK_W)
    r = jax.nn.sigmoid(jnp.einsum('bshi,hij->bshj', xh, w_rg).reshape(bsz, S, W) + b_rg)
    i = jax.nn.sigmoid(jnp.einsum('bshi,hij->bshj', xh, w_ig).reshape(bsz, S, W) + b_ig)
    log_a = -LRU_C * r.astype(jnp.float32) * jax.nn.softplus(-lru_lambda.astype(jnp.float32))
    a = jnp.exp(log_a)
    mult = jnp.sqrt(-jnp.expm1(2.0 * log_a))
    b = mult * (i * xc).astype(jnp.float32)

    def combine(e1, e2):
        a1, b1 = e1
        a2, b2 = e2
        return a1 * a2, a2 * b1 + b2

    _, h = lax.associative_scan(combine, (a, b), axis=1)
    return (h.astype(xr.dtype) * jax.nn.gelu(gate_in))


def hybrid_mixer(x, w_in, conv_w, conv_b, w_rg, b_rg, w_ig, b_ig, lru_lambda, sinks, w_branch, w_out):
    bsz, S, D = x.shape
    proj = x @ w_in
    split_at = list(np.cumsum(IN_WIDTHS)[:-1])
    lru_x, lru_gate, qb, kb, vb, qc, kc, vc, gates = jnp.split(proj, split_at, axis=-1)
    y_a = rglru_branch(lru_x, lru_gate, conv_w, conv_b, w_rg, b_rg, w_ig, b_ig, lru_lambda)
    o_b, _ = banded_window_attention(qb.reshape(bsz, S, SWA_KV_HEADS, SWA_GROUP, HEAD_DIM),
                                     kb.reshape(bsz, S, SWA_KV_HEADS, HEAD_DIM),
                                     vb.reshape(bsz, S, SWA_KV_HEADS, HEAD_DIM),
                                     SWA_WINDOW, sinks)
    y_b = o_b.reshape(bsz, S, SWA_HEADS * HEAD_DIM)
    y_c = dilated_attention(qc.reshape(bsz, S, DIL_HEADS, HEAD_DIM),
                            kc.reshape(bsz, S, DIL_HEADS, HEAD_DIM),
                            vc.reshape(bsz, S, DIL_HEADS, HEAD_DIM)).reshape(bsz, S, DIL_HEADS * HEAD_DIM)
    ys = jnp.stack([y_a, y_b, y_c], axis=2)
    branch = jnp.einsum('bsnc,ncd->bsnd', ys, w_branch)
    merged = jnp.sum(jax.nn.sigmoid(gates.reshape(bsz, S, N_BRANCHES, D)) * branch, axis=2)
    return merged @ w_out


def swiglu(x, w_ffn_in, w_ffn_out):
    h1, h3 = jnp.split(x @ w_ffn_in, 2, axis=-1)
    return (jax.nn.silu(h1) * h3) @ w_ffn_out


def setup_inputs(seed: int = 0) -> dict:
    key = jax.random.key(seed)
    ks = jax.random.split(key, 20)
    f32 = jnp.float32

    def nrm(k, shape, scale):
        return jax.random.normal(k, shape, f32) * scale

    u = jax.random.uniform(ks[7], (DEPTH, LRU_W), f32, 0.9, 0.999)
    return {
        "x": jax.random.normal(ks[0], (BATCH, SEQ, D_MODEL), f32),
        "w_in": nrm(ks[1], (DEPTH, D_MODEL, IN_WIDTH), D_MODEL ** -0.5),
        "conv_w": nrm(ks[2], (DEPTH, CONV_WIDTH, LRU_W), CONV_WIDTH ** -0.5),
        "conv_b": nrm(ks[3], (DEPTH, LRU_W), 0.02),
        "w_rg": nrm(ks[4], (DEPTH, LRU_BLOCKS, LRU_BLOCK_W, LRU_BLOCK_W), LRU_BLOCK_W ** -0.5),
        "b_rg": nrm(ks[5], (DEPTH, LRU_W), 0.02),
        "w_ig": nrm(ks[6], (DEPTH, LRU_BLOCKS, LRU_BLOCK_W, LRU_BLOCK_W), LRU_BLOCK_W ** -0.5),
        "b_ig": nrm(ks[8], (DEPTH, LRU_W), 0.02),
        "lru_lambda": jnp.log(u / (1.0 - u)),
        "sinks": nrm(ks[9], (DEPTH, SWA_HEADS), 0.5),
        "w_branch": nrm(ks[10], (DEPTH, N_BRANCHES, BRANCH_W, D_MODEL), DEEPNORM_BETA * BRANCH_W ** -0.5),
        "w_out": nrm(ks[11], (DEPTH, D_MODEL, D_MODEL), DEEPNORM_BETA * D_MODEL ** -0.5),
        "ln1_g": 1.0 + nrm(ks[12], (DEPTH, D_MODEL), 0.02),
        "ln1_b": nrm(ks[13], (DEPTH, D_MODEL), 0.02),
        "w_ffn_in": nrm(ks[14], (DEPTH, D_MODEL, 2 * FF_HIDDEN), DEEPNORM_BETA * D_MODEL ** -0.5),
        "w_ffn_out": nrm(ks[15], (DEPTH, FF_HIDDEN, D_MODEL), DEEPNORM_BETA * FF_HIDDEN ** -0.5),
        "ln2_g": 1.0 + nrm(ks[16], (DEPTH, D_MODEL), 0.02),
        "ln2_b": nrm(ks[17], (DEPTH, D_MODEL), 0.02),
    }


def reference(x, w_in, conv_w, conv_b, w_rg, b_rg, w_ig, b_ig, lru_lambda, sinks, w_branch, w_out,
              ln1_g, ln1_b, w_ffn_in, w_ffn_out, ln2_g, ln2_b):
    for l in range(DEPTH):
        mix = hybrid_mixer(x, w_in[l], conv_w[l], conv_b[l], w_rg[l], b_rg[l], w_ig[l], b_ig[l],
                           lru_lambda[l], sinks[l], w_branch[l], w_out[l])
        x = layer_norm(DEEPNORM_ALPHA * x + mix, ln1_g[l], ln1_b[l])
        ffn = swiglu(x, w_ffn_in[l], w_ffn_out[l])
        x = layer_norm(DEEPNORM_ALPHA * x + ffn, ln2_g[l], ln2_b[l])
    return x
```

```python
import functools
import math

import jax
import jax.numpy as jnp
from jax import lax
from jax.experimental import pallas as pl
from jax.experimental.pallas import tpu as pltpu

f32 = jnp.float32
bf16 = jnp.bfloat16

D_MODEL = 1024
DEPTH = 4
HEAD_DIM = 64
N_BRANCHES = 3
LRU_BLOCKS = 16
CONV_WIDTH = 4
LRU_C = 8.0
SWA_HEADS = 16
SWA_KV_HEADS = 4
WINDOW = 128
DIL_2, DIL_3 = 4, 16
FF_HIDDEN = 2816
ALPHA = (2.0 * DEPTH) ** 0.25
LN_EPS = 1e-5
NEG_INF = -1e30

LANES = 128
SUBLANES = 8
VMEM_LIMIT = 56 * 1024 * 1024

COL_LRU_X = 0
COL_LRU_G = 1024
COL_QB = 2048
COL_QC = 3072
COL_KC = 4096
COL_VC = 5120
COL_GATES = 6144
COL_KB = 9216
COL_VB = 9472
IN_WIDTH = 9728

PROJ_TM = 512
PROJ_TN = 512
LRU_CW = 256
LRU_T = 128
ROW_TM = 512
FFN_CH = 256


def _cparams(n_grid):
    return pltpu.CompilerParams(dimension_semantics=("arbitrary",) * n_grid, vmem_limit_bytes=VMEM_LIMIT)


def _resident(block_shape, index_map):
    return pl.BlockSpec(block_shape, index_map, pipeline_mode=pl.Buffered(1))


def _proj_kernel(x_ref, w_ref, o_ref):
    xb = x_ref[...].astype(bf16)
    for c in range(IN_WIDTH // PROJ_TN):
        sl = slice(c * PROJ_TN, (c + 1) * PROJ_TN)
        o_ref[:, sl] = jnp.dot(xb, w_ref[:, sl], preferred_element_type=f32).astype(bf16)


def _proj(x, w):
    m = x.shape[0]
    return pl.pallas_call(
        _proj_kernel,
        grid=(m // PROJ_TM,),
        in_specs=[pl.BlockSpec((PROJ_TM, D_MODEL), lambda i: (i, 0)),
                  _resident((D_MODEL, IN_WIDTH), lambda i: (0, 0))],
        out_specs=pl.BlockSpec((PROJ_TM, IN_WIDTH), lambda i: (i, 0)),
        out_shape=jax.ShapeDtypeStruct((m, IN_WIDTH), bf16),
        compiler_params=_cparams(1),
        name="proj",
    )(x, w)


def _lru_kernel(x_ref, g_ref, cw_ref, cb_ref, wr_ref, br_ref, wi_ref, bi_ref, lam_ref, o_ref, xpad_ref):
    seq = x_ref.shape[0]
    groups = LRU_T // SUBLANES
    xpad_ref[0:SUBLANES, :] = jnp.zeros((SUBLANES, LRU_CW), f32)
    xpad_ref[SUBLANES:, :] = x_ref[...].astype(f32)
    cw = cw_ref[...]
    cb = cb_ref[...]
    br = br_ref[...]
    bi = bi_ref[...]
    sp = jax.nn.softplus(-lam_ref[...])
    row8 = lax.broadcasted_iota(jnp.int32, (groups, SUBLANES, LRU_CW), 1)

    def chunk(c, carry):
        t0 = pl.multiple_of(c * LRU_T, LRU_T)
        xe = xpad_ref[pl.ds(t0, LRU_T + SUBLANES), :]
        xc = cb + cw[CONV_WIDTH - 1:CONV_WIDTH] * xe[SUBLANES:]
        for s in range(1, CONV_WIDTH):
            xc = xc + cw[CONV_WIDTH - 1 - s:CONV_WIDTH - s] * pltpu.roll(xe, s, 0)[SUBLANES:]
        xcb = xc.astype(bf16)
        r = jax.nn.sigmoid(jnp.dot(xcb, wr_ref[...], preferred_element_type=f32) + br)
        i = jax.nn.sigmoid(jnp.dot(xcb, wi_ref[...], preferred_element_type=f32) + bi)
        log_a = (-LRU_C * r) * sp
        a = jnp.exp(log_a)
        mult = jnp.sqrt(1.0 - a * a)
        b = mult * (i * xc)
        a3 = a.reshape(groups, SUBLANES, LRU_CW)
        b3 = b.reshape(groups, SUBLANES, LRU_CW)
        for d in (1, 2, 4):
            a_s = jnp.where(row8 >= d, pltpu.roll(a3, d, 1), 1.0)
            b_s = jnp.where(row8 >= d, pltpu.roll(b3, d, 1), 0.0)
            b3 = a3 * b_s + b3
            a3 = a3 * a_s
        hs = []
        for g in range(groups):
            h8 = a3[g] * carry + b3[g]
            carry = h8[SUBLANES - 1:SUBLANES, :]
            hs.append(h8)
        h = jnp.concatenate(hs, axis=0)
        gate = g_ref[pl.ds(t0, LRU_T), :].astype(f32)
        o_ref[pl.ds(t0, LRU_T), :] = (h * jax.nn.gelu(gate)).astype(bf16)
        return carry

    lax.fori_loop(0, seq // LRU_T, chunk, jnp.zeros((1, LRU_CW), f32))


def _lru(proj, cw, cb, wr, br, wi, bi, lam, batch, seq):
    m = batch * seq
    ncg = D_MODEL // LRU_CW
    vec = lambda rows: pl.BlockSpec((rows, LRU_CW), lambda b, c: (0, c))
    mat = pl.BlockSpec((None, LRU_CW, LRU_CW), lambda b, c: (c, 0, 0))
    return pl.pallas_call(
        _lru_kernel,
        grid=(batch, ncg),
        in_specs=[pl.BlockSpec((seq, LRU_CW), lambda b, c: (b, COL_LRU_X // LRU_CW + c)),
                  pl.BlockSpec((seq, LRU_CW), lambda b, c: (b, COL_LRU_G // LRU_CW + c)),
                  vec(CONV_WIDTH), vec(1), mat, vec(1), mat, vec(1), vec(1)],
        out_specs=pl.BlockSpec((seq, LRU_CW), lambda b, c: (b, c)),
        out_shape=jax.ShapeDtypeStruct((m, D_MODEL), bf16),
        scratch_shapes=[pltpu.VMEM((seq + SUBLANES, LRU_CW), f32)],
        compiler_params=_cparams(2),
        name="lru",
    )(proj, proj, cw, cb, wr, br, wi, bi, lam)


def _attend(q2, kk, vv, mask, lane_lo, sinks=None):
    parts = []
    for hh in (0, 1):
        sel = lane_lo if hh == 0 else jnp.logical_not(lane_lo)
        ksel = jnp.where(sel, kk, jnp.zeros_like(kk))
        s = lax.dot_general(q2, ksel, (((1,), (1,)), ((), ())), preferred_element_type=f32)
        s = jnp.where(mask, s, NEG_INF)
        m = jnp.max(s, axis=-1, keepdims=True)
        if sinks is not None:
            m = jnp.maximum(m, sinks[hh])
        p = jnp.exp(s - m)
        den = jnp.sum(p, axis=-1, keepdims=True)
        if sinks is not None:
            den = den + jnp.exp(sinks[hh] - m)
        acc = jnp.dot(p.astype(bf16), vv, preferred_element_type=f32)
        parts.append((acc, m, den))
    (a0, m0, d0), (a1, m1, d1) = parts
    shape = a0.shape
    acc = jnp.where(lane_lo, a0, a1)
    m = jnp.where(lane_lo, jnp.broadcast_to(m0, shape), jnp.broadcast_to(m1, shape))
    den = jnp.where(lane_lo, jnp.broadcast_to(d0, shape), jnp.broadcast_to(d1, shape))
    return acc, m, den


def _band_mask(first_block):
    qi = lax.broadcasted_iota(jnp.int32, (WINDOW, 2 * WINDOW), 0)
    kj = lax.broadcasted_iota(jnp.int32, (WINDOW, 2 * WINDOW), 1)
    rel = qi + WINDOW - kj
    band = jnp.logical_and(rel >= 0, rel <= WINDOW)
    return jnp.logical_and(band, jnp.logical_or(kj >= WINDOW, jnp.logical_not(first_block)))


def _swa_kernel(sink_ref, q_ref, k_ref, v_ref, o_ref):
    seq = q_ref.shape[0]
    kvp = pl.program_id(1)
    lane_lo = lax.broadcasted_iota(jnp.int32, (1, LANES), 1) < HEAD_DIM

    def body(i, _):
        cur = pl.multiple_of(i * WINDOW, WINDOW)
        prev = pl.multiple_of(jnp.maximum(i - 1, 0) * WINDOW, WINDOW)
        mask = _band_mask(i == 0)
        kk = jnp.concatenate([k_ref[pl.ds(prev, WINDOW), :], k_ref[pl.ds(cur, WINDOW), :]], axis=0)
        vv = jnp.concatenate([v_ref[pl.ds(prev, WINDOW), :], v_ref[pl.ds(cur, WINDOW), :]], axis=0)
        kr = pltpu.roll(kk, HEAD_DIM, 1)
        vr = pltpu.roll(vv, HEAD_DIM, 1)
        for half in (0, 1):
            sel = lane_lo if half == 0 else jnp.logical_not(lane_lo)
            kd = jnp.where(sel, kk, kr)
            vd = jnp.where(sel, vv, vr)
            for pr in (0, 1):
                c0 = (half * 2 + pr) * LANES
                head = kvp * 8 + half * 4 + pr * 2
                q2 = q_ref[pl.ds(cur, WINDOW), c0:c0 + LANES]
                acc, _, den = _attend(q2, kd, vd, mask, lane_lo, sinks=(sink_ref[head], sink_ref[head + 1]))
                o_ref[pl.ds(cur, WINDOW), c0:c0 + LANES] = (acc / den).astype(bf16)
        return 0

    lax.fori_loop(0, seq // WINDOW, body, 0)


def _swa(proj, sinks, batch, seq):
    m = batch * seq
    qw = 4 * LANES
    return pl.pallas_call(
        _swa_kernel,
        grid=(batch, 2),
        in_specs=[pl.BlockSpec(memory_space=pltpu.SMEM),
                  pl.BlockSpec((seq, qw), lambda b, p: (b, COL_QB // qw + p)),
                  pl.BlockSpec((seq, LANES), lambda b, p: (b, COL_KB // LANES + p)),
                  pl.BlockSpec((seq, LANES), lambda b, p: (b, COL_VB // LANES + p))],
        out_specs=pl.BlockSpec((seq, qw), lambda b, p: (b, p)),
        out_shape=jax.ShapeDtypeStruct((m, D_MODEL), bf16),
        compiler_params=_cparams(2),
        name="swa",
    )(sinks, proj, proj, proj)


def _dil_kernel(q_ref, k_ref, v_ref, o_ref, qf, kf, vf, acc1, m1, d1, acc2, m2, d2, acc3, m3, d3):
    seq = q_ref.shape[0]
    lane_lo = lax.broadcasted_iota(jnp.int32, (1, LANES), 1) < HEAD_DIM
    qf[...] = q_ref[...].astype(f32)
    kf[...] = k_ref[...].astype(f32)
    vf[...] = v_ref[...].astype(f32)

    def body1(i, _):
        cur = pl.multiple_of(i * WINDOW, WINDOW)
        prev = pl.multiple_of(jnp.maximum(i - 1, 0) * WINDOW, WINDOW)
        q2 = q_ref[pl.ds(cur, WINDOW), :]
        kk = jnp.concatenate([k_ref[pl.ds(prev, WINDOW), :], k_ref[pl.ds(cur, WINDOW), :]], axis=0)
        vv = jnp.concatenate([v_ref[pl.ds(prev, WINDOW), :], v_ref[pl.ds(cur, WINDOW), :]], axis=0)
        acc, m, den = _attend(q2, kk, vv, _band_mask(i == 0), lane_lo)
        acc1[pl.ds(cur, WINDOW), :] = acc
        m1[pl.ds(cur, WINDOW), :] = m
        d1[pl.ds(cur, WINDOW), :] = den
        return 0

    lax.fori_loop(0, seq // WINDOW, body1, 0)

    nb2 = seq // (DIL_2 * WINDOW)

    def body2(idx, _):
        r = idx // nb2
        jb = idx % nb2
        cur = r + DIL_2 * WINDOW * jb
        prev = r + DIL_2 * WINDOW * jnp.maximum(jb - 1, 0)
        rows_c = pl.ds(cur, WINDOW, stride=DIL_2)
        rows_p = pl.ds(prev, WINDOW, stride=DIL_2)
        q2 = qf[rows_c, :].astype(bf16)
        kk = jnp.concatenate([kf[rows_p, :], kf[rows_c, :]], axis=0).astype(bf16)
        vv = jnp.concatenate([vf[rows_p, :], vf[rows_c, :]], axis=0).astype(bf16)
        acc, m, den = _attend(q2, kk, vv, _band_mask(jb == 0), lane_lo)
        acc2[rows_c, :] = acc
        m2[rows_c, :] = m
        d2[rows_c, :] = den
        return 0

    lax.fori_loop(0, DIL_2 * nb2, body2, 0)

    qi = lax.broadcasted_iota(jnp.int32, (WINDOW, WINDOW), 0)
    kj = lax.broadcasted_iota(jnp.int32, (WINDOW, WINDOW), 1)
    causal = jnp.logical_and(qi - kj >= 0, qi - kj <= WINDOW)

    def body3(r, _):
        rows = pl.ds(r, WINDOW, stride=DIL_3)
        q2 = qf[rows, :].astype(bf16)
        kk = kf[rows, :].astype(bf16)
        vv = vf[rows, :].astype(bf16)
        acc, m, den = _attend(q2, kk, vv, causal, lane_lo)
        acc3[rows, :] = acc
        m3[rows, :] = m
        d3[rows, :] = den
        return 0

    lax.fori_loop(0, DIL_3, body3, 0)

    ct = 256

    def combine(c, _):
        rows = pl.ds(pl.multiple_of(c * ct, ct), ct)
        ma, mb, mc = m1[rows, :], m2[rows, :], m3[rows, :]
        mx = jnp.maximum(jnp.maximum(ma, mb), mc)
        wa, wb, wc = jnp.exp(ma - mx), jnp.exp(mb - mx), jnp.exp(mc - mx)
        num = wa * acc1[rows, :] + wb * acc2[rows, :] + wc * acc3[rows, :]
        den = wa * d1[rows, :] + wb * d2[rows, :] + wc * d3[rows, :]
        o_ref[rows, :] = (num / den).astype(bf16)
        return 0

    lax.fori_loop(0, seq // ct, combine, 0)


def _dil(proj, batch, seq):
    m = batch * seq
    npair = D_MODEL // LANES
    assert seq == DIL_3 * WINDOW, "pattern 3 is written for one 128-token block per subsequence"
    spec = lambda col: pl.BlockSpec((seq, LANES), lambda b, p: (b, col // LANES + p))
    return pl.pallas_call(
        _dil_kernel,
        grid=(batch, npair),
        in_specs=[spec(COL_QC), spec(COL_KC), spec(COL_VC)],
        out_specs=pl.BlockSpec((seq, LANES), lambda b, p: (b, p)),
        out_shape=jax.ShapeDtypeStruct((m, D_MODEL), bf16),
        scratch_shapes=[pltpu.VMEM((seq, LANES), f32)] * 12,
        compiler_params=_cparams(2),
        name="dil",
    )(proj, proj, proj)


def _layer_norm(z, g, b):
    mu = jnp.mean(z, axis=-1, keepdims=True)
    zc = z - mu
    var = jnp.mean(zc * zc, axis=-1, keepdims=True)
    return zc * lax.rsqrt(var + LN_EPS) * g + b


def _merge_kernel(x_ref, ya_ref, yb_ref, yc_ref, gates_ref, wb_ref, wo_ref, g_ref, b_ref, o_ref):
    merged = None
    for n, y_ref in enumerate((ya_ref, yb_ref, yc_ref)):
        branch = jnp.dot(y_ref[...], wb_ref[n], preferred_element_type=f32)
        gate = jax.nn.sigmoid(gates_ref[:, n * D_MODEL:(n + 1) * D_MODEL].astype(f32))
        term = gate * branch
        merged = term if merged is None else merged + term
    mix = jnp.dot(merged.astype(bf16), wo_ref[...], preferred_element_type=f32)
    o_ref[...] = _layer_norm(ALPHA * x_ref[...] + mix, g_ref[...], b_ref[...])


def _merge(x, ya, yb, yc, proj, wb, wo, g, b):
    m = x.shape[0]
    row = lambda width, col_block: pl.BlockSpec((ROW_TM, width), lambda i: (i, col_block))
    gw = N_BRANCHES * D_MODEL
    return pl.pallas_call(
        _merge_kernel,
        grid=(m // ROW_TM,),
        in_specs=[row(D_MODEL, 0), row(D_MODEL, 0), row(D_MODEL, 0), row(D_MODEL, 0), row(gw, COL_GATES // gw),
                  _resident((N_BRANCHES, D_MODEL, D_MODEL), lambda i: (0, 0, 0)),
                  _resident((D_MODEL, D_MODEL), lambda i: (0, 0)),
                  _resident((1, D_MODEL), lambda i: (0, 0)),
                  _resident((1, D_MODEL), lambda i: (0, 0))],
        out_specs=row(D_MODEL, 0),
        out_shape=jax.ShapeDtypeStruct((m, D_MODEL), f32),
        compiler_params=_cparams(1),
        name="merge",
    )(x, ya, yb, yc, proj, wb, wo, g, b)


def _ffn_kernel(x_ref, win_ref, wout_ref, g_ref, b_ref, o_ref):
    x = x_ref[...]
    xb = x.astype(bf16)
    acc = None
    for c in range(FF_HIDDEN // FFN_CH):
        h1 = jnp.dot(xb, win_ref[:, c * FFN_CH:(c + 1) * FFN_CH], preferred_element_type=f32)
        h3 = jnp.dot(xb, win_ref[:, FF_HIDDEN + c * FFN_CH:FF_HIDDEN + (c + 1) * FFN_CH], preferred_element_type=f32)
        act = (jax.nn.silu(h1) * h3).astype(bf16)
        part = jnp.dot(act, wout_ref[c * FFN_CH:(c + 1) * FFN_CH, :], preferred_element_type=f32)
        acc = part if acc is None else acc + part
    o_ref[...] = _layer_norm(ALPHA * x + acc, g_ref[...], b_ref[...])


def _ffn(x, win, wout, g, b):
    m = x.shape[0]
    return pl.pallas_call(
        _ffn_kernel,
        grid=(m // ROW_TM,),
        in_specs=[pl.BlockSpec((ROW_TM, D_MODEL), lambda i: (i, 0)),
                  _resident((D_MODEL, 2 * FF_HIDDEN), lambda i: (0, 0)),
                  _resident((FF_HIDDEN, D_MODEL), lambda i: (0, 0)),
                  _resident((1, D_MODEL), lambda i: (0, 0)),
                  _resident((1, D_MODEL), lambda i: (0, 0))],
        out_specs=pl.BlockSpec((ROW_TM, D_MODEL), lambda i: (i, 0)),
        out_shape=jax.ShapeDtypeStruct((m, D_MODEL), f32),
        compiler_params=_cparams(1),
        name="ffn",
    )(x, win, wout, g, b)


def _prep_w_in(w_in):
    scale = HEAD_DIM ** -0.5
    lru_x, lru_g, qb, kb, vb, qc, kc, vc, gates = jnp.split(
        w_in, [1024, 2048, 3072, 3328, 3584, 4608, 5632, 6656], axis=-1)
    w = jnp.concatenate([lru_x, lru_g, qb * scale, qc * scale, kc, vc, gates, kb, vb], axis=-1)
    return w.astype(bf16)


def _block_diag(w):
    per = LRU_CW // HEAD_DIM
    w = w.reshape(DEPTH, LRU_BLOCKS // per, per, HEAD_DIM, HEAD_DIM)
    eye = jnp.eye(per, dtype=w.dtype)
    bd = jnp.einsum("lgaij,ab->lgaibj", w, eye)
    return bd.reshape(DEPTH, LRU_BLOCKS // per, LRU_CW, LRU_CW).astype(bf16)


def kernel(x, w_in, conv_w, conv_b, w_rg, b_rg, w_ig, b_ig, lru_lambda, sinks, w_branch, w_out,
           ln1_g, ln1_b, w_ffn_in, w_ffn_out, ln2_g, ln2_b):
    batch, seq, d = x.shape
    assert d == D_MODEL and w_in.shape == (DEPTH, D_MODEL, IN_WIDTH)
    w_in_b = _prep_w_in(w_in)
    wr_b = _block_diag(w_rg)
    wi_b = _block_diag(w_ig)
    wb_b = w_branch.astype(bf16)
    wo_b = w_out.astype(bf16)
    wfi_b = w_ffn_in.astype(bf16)
    wfo_b = w_ffn_out.astype(bf16)
    row = lambda p, l: p[l].reshape(1, D_MODEL)

    h = x.reshape(batch * seq, D_MODEL)
    for l in range(DEPTH):
        proj = _proj(h, w_in_b[l])
        ya = _lru(proj, conv_w[l], row(conv_b, l), wr_b[l], row(b_rg, l), wi_b[l], row(b_ig, l),
                  row(lru_lambda, l), batch, seq)
        yb = _swa(proj, sinks[l], batch, seq)
        yc = _dil(proj, batch, seq)
        h = _merge(h, ya, yb, yc, proj, wb_b[l], wo_b[l], row(ln1_g, l), row(ln1_b, l))
        h = _ffn(h, wfi_b[l], wfo_b[l], row(ln2_g, l), row(ln2_b, l))
    return h.reshape(batch, seq, D_MODEL)
```

```python
import functools
import math

import jax
import jax.numpy as jnp
from jax import lax
from jax.experimental import pallas as pl
from jax.experimental.pallas import tpu as pltpu

f32 = jnp.float32
bf16 = jnp.bfloat16

D_MODEL = 1024
DEPTH = 4
HEAD_DIM = 64
N_BRANCHES = 3
LRU_BLOCKS = 16
CONV_WIDTH = 4
LRU_C = 8.0
SWA_HEADS = 16
SWA_KV_HEADS = 4
WINDOW = 128
DIL_2, DIL_3 = 4, 16
FF_HIDDEN = 2816
ALPHA = (2.0 * DEPTH) ** 0.25
LN_EPS = 1e-5
NEG_INF = -1e30
LOG2E = 1.4426950408889634

LANES = 128
SUBLANES = 8
VMEM_LIMIT = 56 * 1024 * 1024

COL_LRU_X = 0
COL_LRU_G = 1024
COL_QB = 2048
COL_QC = 3072
COL_KC = 4096
COL_VC = 5120
COL_GATES = 6144
COL_KB = 9216
COL_VB = 9472
IN_WIDTH = 9728

PROJ_TM = 512
PROJ_TN = 512
LRU_CW = 256
LRU_T = 128
ROW_TM = 512
FFN_CH = 256


def _cparams(n_grid):
    return pltpu.CompilerParams(dimension_semantics=("arbitrary",) * n_grid, vmem_limit_bytes=VMEM_LIMIT)


def _resident(block_shape, index_map):
    return pl.BlockSpec(block_shape, index_map, pipeline_mode=pl.Buffered(1))


def _proj_kernel(x_ref, w_ref, o_ref):
    xb = x_ref[...].astype(bf16)
    for c in range(IN_WIDTH // PROJ_TN):
        sl = slice(c * PROJ_TN, (c + 1) * PROJ_TN)
        o_ref[:, sl] = jnp.dot(xb, w_ref[:, sl], preferred_element_type=f32).astype(bf16)


def _proj(x, w):
    m = x.shape[0]
    return pl.pallas_call(
        _proj_kernel,
        grid=(m // PROJ_TM,),
        in_specs=[pl.BlockSpec((PROJ_TM, D_MODEL), lambda i: (i, 0)),
                  _resident((D_MODEL, IN_WIDTH), lambda i: (0, 0))],
        out_specs=pl.BlockSpec((PROJ_TM, IN_WIDTH), lambda i: (i, 0)),
        out_shape=jax.ShapeDtypeStruct((m, IN_WIDTH), bf16),
        compiler_params=_cparams(1),
        name="proj",
    )(x, w)


def _lru_kernel(x_ref, g_ref, cw_ref, cb_ref, wr_ref, br_ref, wi_ref, bi_ref, lam_ref, o_ref, xpad_ref):
    seq = x_ref.shape[0]
    groups = LRU_T // SUBLANES
    xpad_ref[0:SUBLANES, :] = jnp.zeros((SUBLANES, LRU_CW), f32)
    xpad_ref[SUBLANES:, :] = x_ref[...].astype(f32)
    cw = cw_ref[...]
    cb = cb_ref[...]
    br = br_ref[...]
    bi = bi_ref[...]
    sp = jax.nn.softplus(-lam_ref[...])
    row8 = lax.broadcasted_iota(jnp.int32, (groups, SUBLANES, LRU_CW), 1)

    def chunk(c, carry):
        t0 = pl.multiple_of(c * LRU_T, LRU_T)
        xe = xpad_ref[pl.ds(t0, LRU_T + SUBLANES), :]
        xc = cb + cw[CONV_WIDTH - 1:CONV_WIDTH] * xe[SUBLANES:]
        for s in range(1, CONV_WIDTH):
            xc = xc + cw[CONV_WIDTH - 1 - s:CONV_WIDTH - s] * pltpu.roll(xe, s, 0)[SUBLANES:]
        xcb = xc.astype(bf16)
        r = jax.nn.sigmoid(jnp.dot(xcb, wr_ref[...], preferred_element_type=f32) + br)
        i = jax.nn.sigmoid(jnp.dot(xcb, wi_ref[...], preferred_element_type=f32) + bi)
        log_a = (-LRU_C * r) * sp
        a = jnp.exp(log_a)
        mult = jnp.sqrt(1.0 - a * a)
        b = mult * (i * xc)
        a3 = a.reshape(groups, SUBLANES, LRU_CW)
        b3 = b.reshape(groups, SUBLANES, LRU_CW)
        for d in (1, 2, 4):
            a_s = jnp.where(row8 >= d, pltpu.roll(a3, d, 1), 1.0)
            b_s = jnp.where(row8 >= d, pltpu.roll(b3, d, 1), 0.0)
            b3 = a3 * b_s + b3
            a3 = a3 * a_s
        hs = []
        for g in range(groups):
            h8 = a3[g] * carry + b3[g]
            carry = h8[SUBLANES - 1:SUBLANES, :]
            hs.append(h8)
        h = jnp.concatenate(hs, axis=0)
        gate = g_ref[pl.ds(t0, LRU_T), :].astype(f32)
        o_ref[pl.ds(t0, LRU_T), :] = (h * jax.nn.gelu(gate)).astype(bf16)
        return carry

    lax.fori_loop(0, seq // LRU_T, chunk, jnp.zeros((1, LRU_CW), f32))


def _lru(proj, cw, cb, wr, br, wi, bi, lam, batch, seq):
    m = batch * seq
    ncg = D_MODEL // LRU_CW
    vec = lambda rows: pl.BlockSpec((rows, LRU_CW), lambda b, c: (0, c))
    mat = pl.BlockSpec((None, LRU_CW, LRU_CW), lambda b, c: (c, 0, 0))
    return pl.pallas_call(
        _lru_kernel,
        grid=(batch, ncg),
        in_specs=[pl.BlockSpec((seq, LRU_CW), lambda b, c: (b, COL_LRU_X // LRU_CW + c)),
                  pl.BlockSpec((seq, LRU_CW), lambda b, c: (b, COL_LRU_G // LRU_CW + c)),
                  vec(CONV_WIDTH), vec(1), mat, vec(1), mat, vec(1), vec(1)],
        out_specs=pl.BlockSpec((seq, LRU_CW), lambda b, c: (b, c)),
        out_shape=jax.ShapeDtypeStruct((m, D_MODEL), bf16),
        scratch_shapes=[pltpu.VMEM((seq + SUBLANES, LRU_CW), f32)],
        compiler_params=_cparams(2),
        name="lru",
    )(proj, proj, cw, cb, wr, br, wi, bi, lam)


def _band_bias(first_block):
    qi = lax.broadcasted_iota(jnp.int32, (WINDOW, 2 * WINDOW), 0)
    kj = lax.broadcasted_iota(jnp.int32, (WINDOW, 2 * WINDOW), 1)
    rel = qi + WINDOW - kj
    ok = jnp.logical_and(rel >= 0, rel <= WINDOW)
    if first_block:
        ok = jnp.logical_and(ok, kj >= WINDOW)
    return jnp.where(ok, 0.0, NEG_INF).astype(f32)


def _causal_bias():
    qi = lax.broadcasted_iota(jnp.int32, (WINDOW, WINDOW), 0)
    kj = lax.broadcasted_iota(jnp.int32, (WINDOW, WINDOW), 1)
    return jnp.where(qi >= kj, 0.0, NEG_INF).astype(f32)


def _pairs(items, lane_lo):
    nt = (((1,), (1,)), ((), ()))
    scores = [[lax.dot_general(q2, k, nt, preferred_element_type=f32) for k in (k0, k1)]
              for (q2, k0, _, k1, _, _, _) in items]
    probs, maxes = [], []
    for (_, _, _, _, _, bias, sinks), pair_scores in zip(items, scores):
        ps, ms = [], []
        for h, s in enumerate(pair_scores):
            s = s + bias
            m = jnp.max(s, axis=-1, keepdims=True)
            if sinks is not None:
                m = jnp.maximum(m, sinks[h])
            ps.append(jnp.exp2(s - m).astype(bf16))
            ms.append(m)
        probs.append(ps)
        maxes.append(ms)
    accs = [[jnp.dot(p, v, preferred_element_type=f32) for p, v in zip(ps, (v0, v1))]
            for (_, _, v0, _, v1, _, _), ps in zip(items, probs)]
    out = []
    for (_, _, _, _, _, _, sinks), (a0, a1), (m0, m1) in zip(items, accs, maxes):
        num = jnp.where(lane_lo, a0, a1)
        den = pltpu.roll(jnp.where(lane_lo, a1, a0), HEAD_DIM, 1)
        if sinks is not None:
            den = den + jnp.where(lane_lo, jnp.exp2(sinks[0] - m0), jnp.exp2(sinks[1] - m1))
        out.append((num, den, jnp.where(lane_lo, m0, m1)))
    return out


def _split_heads(kk, vv, lane_lo):
    lane_hi = jnp.logical_not(lane_lo)
    zero = jnp.zeros_like(kk)
    one = jnp.ones_like(vv)
    return (jnp.where(lane_lo, kk, zero), jnp.where(lane_lo, vv, one),
            jnp.where(lane_hi, kk, zero), jnp.where(lane_hi, vv, one))


def _swa_kernel(sink_ref, q_ref, k_ref, v_ref, o_ref, bias_ref):
    seq = q_ref.shape[0]
    kvp = pl.program_id(1)
    lane_lo = lax.broadcasted_iota(jnp.int32, (1, LANES), 1) < HEAD_DIM
    lane_hi = jnp.logical_not(lane_lo)
    bias_ref[0] = _band_bias(False)
    bias_ref[1] = _band_bias(True)

    def body(i, _):
        cur = pl.multiple_of(i * WINDOW, WINDOW)
        prev = pl.multiple_of(jnp.maximum(i - 1, 0) * WINDOW, WINDOW)
        bias = bias_ref[jnp.where(i == 0, 1, 0)]
        kk = jnp.concatenate([k_ref[pl.ds(prev, WINDOW), :], k_ref[pl.ds(cur, WINDOW), :]], axis=0)
        vv = jnp.concatenate([v_ref[pl.ds(prev, WINDOW), :], v_ref[pl.ds(cur, WINDOW), :]], axis=0)
        kr = pltpu.roll(kk, HEAD_DIM, 1)
        vr = pltpu.roll(vv, HEAD_DIM, 1)
        zero = jnp.zeros_like(kk)
        one = jnp.ones_like(vv)
        items = []
        for half in (0, 1):
            k_in = (kk, kr) if half == 0 else (kr, kk)
            v_in = (vv, vr) if half == 0 else (vr, vv)
            k0, v0 = jnp.where(lane_lo, k_in[0], zero), jnp.where(lane_lo, v_in[0], one)
            k1, v1 = jnp.where(lane_hi, k_in[1], zero), jnp.where(lane_hi, v_in[1], one)
            for pr in (0, 1):
                c0 = (half * 2 + pr) * LANES
                head = kvp * 8 + half * 4 + pr * 2
                q2 = q_ref[pl.ds(cur, WINDOW), c0:c0 + LANES]
                sinks = (sink_ref[head] * LOG2E, sink_ref[head + 1] * LOG2E)
                items.append((q2, k0, v0, k1, v1, bias, sinks))
        for j, (num, den, _) in enumerate(_pairs(items, lane_lo)):
            o_ref[pl.ds(cur, WINDOW), j * LANES:(j + 1) * LANES] = (num / den).astype(bf16)
        return 0

    lax.fori_loop(0, seq // WINDOW, body, 0)


def _swa(proj, sinks, batch, seq):
    m = batch * seq
    qw = 4 * LANES
    return pl.pallas_call(
        _swa_kernel,
        grid=(batch, 2),
        in_specs=[pl.BlockSpec(memory_space=pltpu.SMEM),
                  pl.BlockSpec((seq, qw), lambda b, p: (b, COL_QB // qw + p)),
                  pl.BlockSpec((seq, LANES), lambda b, p: (b, COL_KB // LANES + p)),
                  pl.BlockSpec((seq, LANES), lambda b, p: (b, COL_VB // LANES + p))],
        out_specs=pl.BlockSpec((seq, qw), lambda b, p: (b, p)),
        out_shape=jax.ShapeDtypeStruct((m, D_MODEL), bf16),
        scratch_shapes=[pltpu.VMEM((2, WINDOW, 2 * WINDOW), f32)],
        compiler_params=_cparams(2),
        name="swa",
    )(sinks, proj, proj, proj)


def _dil_kernel(q_ref, k_ref, v_ref, o_ref, qf, kf, vf, num1, den1, m1, num2, den2, m2, num3, den3, m3,
                bias_ref, causal_ref):
    seq = q_ref.shape[0]
    lane_lo = lax.broadcasted_iota(jnp.int32, (1, LANES), 1) < HEAD_DIM
    bias_ref[0] = _band_bias(False)
    bias_ref[1] = _band_bias(True)
    causal_ref[...] = _causal_bias()
    qf[...] = q_ref[...].astype(f32)
    kf[...] = k_ref[...].astype(f32)
    vf[...] = v_ref[...].astype(f32)
    nb2 = seq // (DIL_2 * WINDOW)

    def body(i, _):
        cur = pl.multiple_of(i * WINDOW, WINDOW)
        prev = pl.multiple_of(jnp.maximum(i - 1, 0) * WINDOW, WINDOW)
        q2 = q_ref[pl.ds(cur, WINDOW), :]
        kk = jnp.concatenate([k_ref[pl.ds(prev, WINDOW), :], k_ref[pl.ds(cur, WINDOW), :]], axis=0)
        vv = jnp.concatenate([v_ref[pl.ds(prev, WINDOW), :], v_ref[pl.ds(cur, WINDOW), :]], axis=0)
        item1 = (q2, *_split_heads(kk, vv, lane_lo), bias_ref[jnp.where(i == 0, 1, 0)], None)

        r = i // nb2
        jb = i % nb2
        cur2 = r + DIL_2 * WINDOW * jb
        prev2 = r + DIL_2 * WINDOW * jnp.maximum(jb - 1, 0)
        rows_c = pl.ds(cur2, WINDOW, stride=DIL_2)
        rows_p = pl.ds(prev2, WINDOW, stride=DIL_2)
        q2 = qf[rows_c, :].astype(bf16)
        kk = jnp.concatenate([kf[rows_p, :], kf[rows_c, :]], axis=0).astype(bf16)
        vv = jnp.concatenate([vf[rows_p, :], vf[rows_c, :]], axis=0).astype(bf16)
        item2 = (q2, *_split_heads(kk, vv, lane_lo), bias_ref[jnp.where(jb == 0, 1, 0)], None)

        rows = pl.ds(i, WINDOW, stride=DIL_3)
        q2 = qf[rows, :].astype(bf16)
        kk = kf[rows, :].astype(bf16)
        vv = vf[rows, :].astype(bf16)
        item3 = (q2, *_split_heads(kk, vv, lane_lo), causal_ref[...], None)

        dests = ((num1, den1, m1, pl.ds(cur, WINDOW)), (num2, den2, m2, rows_c), (num3, den3, m3, rows))
        for (num, den, m), (num_ref, den_ref, m_ref, where) in zip(_pairs([item1, item2, item3], lane_lo), dests):
            num_ref[where, :] = num
            den_ref[where, :] = den
            m_ref[where, :] = m
        return 0

    lax.fori_loop(0, seq // WINDOW, body, 0)

    ct = 256

    def combine(c, _):
        rows = pl.ds(pl.multiple_of(c * ct, ct), ct)
        ma, mb, mc = m1[rows, :], m2[rows, :], m3[rows, :]
        mx = jnp.maximum(jnp.maximum(ma, mb), mc)
        wa, wb, wc = jnp.exp2(ma - mx), jnp.exp2(mb - mx), jnp.exp2(mc - mx)
        num = wa * num1[rows, :] + wb * num2[rows, :] + wc * num3[rows, :]
        den = wa * den1[rows, :] + wb * den2[rows, :] + wc * den3[rows, :]
        o_ref[rows, :] = (num / den).astype(bf16)
        return 0

    lax.fori_loop(0, seq // ct, combine, 0)


def _dil(proj, batch, seq):
    m = batch * seq
    npair = D_MODEL // LANES
    assert seq == DIL_3 * WINDOW, "pattern 3 is written for one 128-token block per subsequence"
    spec = lambda col: pl.BlockSpec((seq, LANES), lambda b, p: (b, col // LANES + p))
    return pl.pallas_call(
        _dil_kernel,
        grid=(batch, npair),
        in_specs=[spec(COL_QC), spec(COL_KC), spec(COL_VC)],
        out_specs=pl.BlockSpec((seq, LANES), lambda b, p: (b, p)),
        out_shape=jax.ShapeDtypeStruct((m, D_MODEL), bf16),
        scratch_shapes=[pltpu.VMEM((seq, LANES), f32)] * 12
        + [pltpu.VMEM((2, WINDOW, 2 * WINDOW), f32), pltpu.VMEM((WINDOW, WINDOW), f32)],
        compiler_params=_cparams(2),
        name="dil",
    )(proj, proj, proj)


def _layer_norm(z, g, b):
    mu = jnp.mean(z, axis=-1, keepdims=True)
    zc = z - mu
    var = jnp.mean(zc * zc, axis=-1, keepdims=True)
    return zc * lax.rsqrt(var + LN_EPS) * g + b


def _merge_kernel(x_ref, ya_ref, yb_ref, yc_ref, gates_ref, wb_ref, wo_ref, g_ref, b_ref, o_ref):
    merged = None
    for n, y_ref in enumerate((ya_ref, yb_ref, yc_ref)):
        branch = jnp.dot(y_ref[...], wb_ref[n], preferred_element_type=f32)
        gate = jax.nn.sigmoid(gates_ref[:, n * D_MODEL:(n + 1) * D_MODEL].astype(f32))
        term = gate * branch
        merged = term if merged is None else merged + term
    mix = jnp.dot(merged.astype(bf16), wo_ref[...], preferred_element_type=f32)
    o_ref[...] = _layer_norm(ALPHA * x_ref[...] + mix, g_ref[...], b_ref[...])


def _merge(x, ya, yb, yc, proj, wb, wo, g, b):
    m = x.shape[0]
    row = lambda width, col_block: pl.BlockSpec((ROW_TM, width), lambda i: (i, col_block))
    gw = N_BRANCHES * D_MODEL
    return pl.pallas_call(
        _merge_kernel,
        grid=(m // ROW_TM,),
        in_specs=[row(D_MODEL, 0), row(D_MODEL, 0), row(D_MODEL, 0), row(D_MODEL, 0), row(gw, COL_GATES // gw),
                  _resident((N_BRANCHES, D_MODEL, D_MODEL), lambda i: (0, 0, 0)),
                  _resident((D_MODEL, D_MODEL), lambda i: (0, 0)),
                  _resident((1, D_MODEL), lambda i: (0, 0)),
                  _resident((1, D_MODEL), lambda i: (0, 0))],
        out_specs=row(D_MODEL, 0),
        out_shape=jax.ShapeDtypeStruct((m, D_MODEL), f32),
        compiler_params=_cparams(1),
        name="merge",
    )(x, ya, yb, yc, proj, wb, wo, g, b)


def _ffn_kernel(x_ref, win_ref, wout_ref, g_ref, b_ref, o_ref):
    x = x_ref[...]
    xb = x.astype(bf16)
    acc = None
    for c in range(FF_HIDDEN // FFN_CH):
        h1 = jnp.dot(xb, win_ref[:, c * FFN_CH:(c + 1) * FFN_CH], preferred_element_type=f32)
        h3 = jnp.dot(xb, win_ref[:, FF_HIDDEN + c * FFN_CH:FF_HIDDEN + (c + 1) * FFN_CH], preferred_element_type=f32)
        act = (jax.nn.silu(h1) * h3).astype(bf16)
        part = jnp.dot(act, wout_ref[c * FFN_CH:(c + 1) * FFN_CH, :], preferred_element_type=f32)
        acc = part if acc is None else acc + part
    o_ref[...] = _layer_norm(ALPHA * x + acc, g_ref[...], b_ref[...])


def _ffn(x, win, wout, g, b):
    m = x.shape[0]
    return pl.pallas_call(
        _ffn_kernel,
        grid=(m // ROW_TM,),
        in_specs=[pl.BlockSpec((ROW_TM, D_MODEL), lambda i: (i, 0)),
                  _resident((D_MODEL, 2 * FF_HIDDEN), lambda i: (0, 0)),
                  _resident((FF_HIDDEN, D_MODEL), lambda i: (0, 0)),
                  _resident((1, D_MODEL), lambda i: (0, 0)),
                  _resident((1, D_MODEL), lambda i: (0, 0))],
        out_specs=pl.BlockSpec((ROW_TM, D_MODEL), lambda i: (i, 0)),
        out_shape=jax.ShapeDtypeStruct((m, D_MODEL), f32),
        compiler_params=_cparams(1),
        name="ffn",
    )(x, win, wout, g, b)


def _prep_w_in(w_in):
    scale = HEAD_DIM ** -0.5 * LOG2E
    lru_x, lru_g, qb, kb, vb, qc, kc, vc, gates = jnp.split(
        w_in, [1024, 2048, 3072, 3328, 3584, 4608, 5632, 6656], axis=-1)
    w = jnp.concatenate([lru_x, lru_g, qb * scale, qc * scale, kc, vc, gates, kb, vb], axis=-1)
    return w.astype(bf16)


def _block_diag(w):
    per = LRU_CW // HEAD_DIM
    w = w.reshape(DEPTH, LRU_BLOCKS // per, per, HEAD_DIM, HEAD_DIM)
    eye = jnp.eye(per, dtype=w.dtype)
    bd = jnp.einsum("lgaij,ab->lgaibj", w, eye)
    return bd.reshape(DEPTH, LRU_BLOCKS // per, LRU_CW, LRU_CW).astype(bf16)


def kernel(x, w_in, conv_w, conv_b, w_rg, b_rg, w_ig, b_ig, lru_lambda, sinks, w_branch, w_out,
           ln1_g, ln1_b, w_ffn_in, w_ffn_out, ln2_g, ln2_b):
    batch, seq, d = x.shape
    assert d == D_MODEL and w_in.shape == (DEPTH, D_MODEL, IN_WIDTH)
    w_in_b = _prep_w_in(w_in)
    wr_b = _block_diag(w_rg)
    wi_b = _block_diag(w_ig)
    wb_b = w_branch.astype(bf16)
    wo_b = w_out.astype(bf16)
    wfi_b = w_ffn_in.astype(bf16)
    wfo_b = w_ffn_out.astype(bf16)
    row = lambda p, l: p[l].reshape(1, D_MODEL)

    h = x.reshape(batch * seq, D_MODEL)
    for l in range(DEPTH):
        proj = _proj(h, w_in_b[l])
        ya = _lru(proj, conv_w[l], row(conv_b, l), wr_b[l], row(b_rg, l), wi_b[l], row(b_ig, l),
                  row(lru_lambda, l), batch, seq)
        yb = _swa(proj, sinks[l], batch, seq)
        yc = _dil(proj, batch, seq)
        h = _merge(h, ya, yb, yc, proj, wb_b[l], wo_b[l], row(ln1_g, l), row(ln1_b, l))
        h = _ffn(h, wfi_b[l], wfo_b[l], row(ln2_g, l), row(ln2_b, l))
    return h.reshape(batch, seq, D_MODEL)
```

```python
import jax
import jax.numpy as jnp
from jax import lax
from jax.experimental import pallas as pl
from jax.experimental.pallas import tpu as pltpu

f32 = jnp.float32
bf16 = jnp.bfloat16

D_MODEL = 1024
DEPTH = 4
HEAD_DIM = 64
N_BRANCHES = 3
LRU_BLOCKS = 16
CONV_WIDTH = 4
LRU_C = 8.0
WINDOW = 128
DIL_2, DIL_3 = 4, 16
FF_HIDDEN = 2816
ALPHA = (2.0 * DEPTH) ** 0.25
LN_EPS = 1e-5
NEG_INF = -1e30
LOG2E = 1.4426950408889634

LANES = 128
SUBLANES = 8
VMEM_LIMIT = 56 * 1024 * 1024

COL_LRU_X = 0
COL_LRU_G = 1024
COL_QB = 2048
COL_QC = 3072
COL_KC = 4096
COL_VC = 5120
COL_GATES = 6144
COL_KB = 9216
COL_VB = 9472
IN_WIDTH = 9728

PROJ_TM = 512
PROJ_TN = 512
LRU_CW = 256
LRU_T = 128
ROW_TM = 512
FFN_CH = 256
SWA_PAIRS = 4
DIL_PATTERNS = 3


def _cparams(n_grid):
    return pltpu.CompilerParams(dimension_semantics=("arbitrary",) * n_grid, vmem_limit_bytes=VMEM_LIMIT)


def _resident(block_shape, index_map):
    return pl.BlockSpec(block_shape, index_map, pipeline_mode=pl.Buffered(1))


def _proj_kernel(x_ref, w_ref, o_ref):
    xb = x_ref[...].astype(bf16)
    for c in range(IN_WIDTH // PROJ_TN):
        sl = slice(c * PROJ_TN, (c + 1) * PROJ_TN)
        o_ref[:, sl] = jnp.dot(xb, w_ref[:, sl], preferred_element_type=f32).astype(bf16)


def _proj(x, w):
    m = x.shape[0]
    return pl.pallas_call(
        _proj_kernel,
        grid=(m // PROJ_TM,),
        in_specs=[pl.BlockSpec((PROJ_TM, D_MODEL), lambda i: (i, 0)),
                  _resident((D_MODEL, IN_WIDTH), lambda i: (0, 0))],
        out_specs=pl.BlockSpec((PROJ_TM, IN_WIDTH), lambda i: (i, 0)),
        out_shape=jax.ShapeDtypeStruct((m, IN_WIDTH), bf16),
        compiler_params=_cparams(1),
        name="proj",
    )(x, w)


def _lru_kernel(x_ref, g_ref, cw_ref, cb_ref, wr_ref, br_ref, wi_ref, bi_ref, lam_ref, o_ref, xpad_ref):
    seq = x_ref.shape[0]
    groups = LRU_T // SUBLANES
    xpad_ref[0:SUBLANES, :] = jnp.zeros((SUBLANES, LRU_CW), f32)
    xpad_ref[SUBLANES:, :] = x_ref[...].astype(f32)
    cw = cw_ref[...]
    cb = cb_ref[...]
    br = br_ref[...]
    bi = bi_ref[...]
    sp = jax.nn.softplus(-lam_ref[...])
    row8 = lax.broadcasted_iota(jnp.int32, (groups, SUBLANES, LRU_CW), 1)

    def chunk(c, carry):
        t0 = pl.multiple_of(c * LRU_T, LRU_T)
        xe = xpad_ref[pl.ds(t0, LRU_T + SUBLANES), :]
        xc = cb + cw[CONV_WIDTH - 1:CONV_WIDTH] * xe[SUBLANES:]
        for s in range(1, CONV_WIDTH):
            xc = xc + cw[CONV_WIDTH - 1 - s:CONV_WIDTH - s] * pltpu.roll(xe, s, 0)[SUBLANES:]
        xcb = xc.astype(bf16)
        r = jax.nn.sigmoid(jnp.dot(xcb, wr_ref[...], preferred_element_type=f32) + br)
        i = jax.nn.sigmoid(jnp.dot(xcb, wi_ref[...], preferred_element_type=f32) + bi)
        log_a = (-LRU_C * r) * sp
        a = jnp.exp(log_a)
        mult = jnp.sqrt(1.0 - a * a)
        b = mult * (i * xc)
        a3 = a.reshape(groups, SUBLANES, LRU_CW)
        b3 = b.reshape(groups, SUBLANES, LRU_CW)
        for d in (1, 2, 4):
            a_s = jnp.where(row8 >= d, pltpu.roll(a3, d, 1), 1.0)
            b_s = jnp.where(row8 >= d, pltpu.roll(b3, d, 1), 0.0)
            b3 = a3 * b_s + b3
            a3 = a3 * a_s
        hs = []
        for g in range(groups):
            h8 = a3[g] * carry + b3[g]
            carry = h8[SUBLANES - 1:SUBLANES, :]
            hs.append(h8)
        h = jnp.concatenate(hs, axis=0)
        gate = g_ref[pl.ds(t0, LRU_T), :].astype(f32)
        o_ref[pl.ds(t0, LRU_T), :] = (h * jax.nn.gelu(gate)).astype(bf16)
        return carry

    lax.fori_loop(0, seq // LRU_T, chunk, jnp.zeros((1, LRU_CW), f32))


def _lru(proj, cw, cb, wr, br, wi, bi, lam, batch, seq):
    m = batch * seq
    ncg = D_MODEL // LRU_CW
    vec = lambda rows: pl.BlockSpec((rows, LRU_CW), lambda b, c: (0, c))
    mat = pl.BlockSpec((None, LRU_CW, LRU_CW), lambda b, c: (c, 0, 0))
    return pl.pallas_call(
        _lru_kernel,
        grid=(batch, ncg),
        in_specs=[pl.BlockSpec((seq, LRU_CW), lambda b, c: (b, COL_LRU_X // LRU_CW + c)),
                  pl.BlockSpec((seq, LRU_CW), lambda b, c: (b, COL_LRU_G // LRU_CW + c)),
                  vec(CONV_WIDTH), vec(1), mat, vec(1), mat, vec(1), vec(1)],
        out_specs=pl.BlockSpec((seq, LRU_CW), lambda b, c: (b, c)),
        out_shape=jax.ShapeDtypeStruct((m, D_MODEL), bf16),
        scratch_shapes=[pltpu.VMEM((seq + SUBLANES, LRU_CW), f32)],
        compiler_params=_cparams(2),
        name="lru",
    )(proj, proj, cw, cb, wr, br, wi, bi, lam)


def _band_bias(first_block):
    qi = lax.broadcasted_iota(jnp.int32, (WINDOW, 2 * WINDOW), 0)
    kj = lax.broadcasted_iota(jnp.int32, (WINDOW, 2 * WINDOW), 1)
    rel = qi + WINDOW - kj
    ok = jnp.logical_and(rel >= 0, rel <= WINDOW)
    if first_block:
        ok = jnp.logical_and(ok, kj >= WINDOW)
    return jnp.where(ok, 0.0, NEG_INF).astype(f32)


def _causal_bias():
    qi = lax.broadcasted_iota(jnp.int32, (WINDOW, WINDOW), 0)
    kj = lax.broadcasted_iota(jnp.int32, (WINDOW, WINDOW), 1)
    return jnp.where(qi >= kj, 0.0, NEG_INF).astype(f32)


def _nt_dot(a, b):
    return lax.dot_general(a, b, (((1,), (1,)), ((), ())), preferred_element_type=f32)


def _lane_lo():
    return lax.broadcasted_iota(jnp.int32, (1, LANES), 1) < HEAD_DIM


def _per_head(x, fill):
    lo = _lane_lo()
    f = jnp.full_like(x, fill)
    return jnp.where(lo, x, f), jnp.where(lo, f, x)


def _join_pair(a0, a1):
    lo = _lane_lo()
    return jnp.where(lo, a0, a1), pltpu.roll(jnp.where(lo, a1, a0), HEAD_DIM, 1)


def _run_pipeline(nblk, qk, softmax, pv):
    assert nblk % 2 == 0 and nblk >= 4
    qk(0, 0)
    qk(1, 1)
    softmax(0, 0)

    def two_steps(i, _):
        for slot in (0, 1):
            t = 2 * i + slot
            pv(t - 2, slot)
            qk(t, slot)
            softmax(t - 1, 1 - slot)
        return 0

    lax.fori_loop(1, nblk // 2, two_steps, 0)
    pv(nblk - 2, 0)
    softmax(nblk - 1, 1)
    pv(nblk - 1, 1)


def _block_rows(j):
    if isinstance(j, int):
        return pl.ds(max(j - 1, 0) * WINDOW, WINDOW), pl.ds(j * WINDOW, WINDOW)
    cur = pl.multiple_of(j * WINDOW, WINDOW)
    prev = pl.multiple_of(jnp.maximum(j - 1, 0) * WINDOW, WINDOW)
    return pl.ds(prev, WINDOW), pl.ds(cur, WINDOW)


def _swa_kernel(sink_ref, q_ref, k_ref, v_ref, o_ref, bias_ref, s_buf, p_buf, e_buf):
    seq = q_ref.shape[0]
    kvp = pl.program_id(1)
    lo = _lane_lo()
    bias_ref[0] = _band_bias(False)
    bias_ref[1] = _band_bias(True)

    def kv_per_query_head(ref, j, fill):
        prev, cur = _block_rows(j)
        xx = jnp.concatenate([ref[prev, :], ref[cur, :]], axis=0)
        xr = pltpu.roll(xx, HEAD_DIM, 1)
        f = jnp.full_like(xx, fill)
        return ((jnp.where(lo, xx, f), jnp.where(lo, f, xr)),
                (jnp.where(lo, xr, f), jnp.where(lo, f, xx)))

    for kvh in (0, 1):
        pairs = (2 * kvh, 2 * kvh + 1)

        def qk(j, slot, kvh=kvh, pairs=pairs):
            _, cur = _block_rows(j)
            bias = bias_ref[jnp.where(j == 0, 1, 0)]
            ks = kv_per_query_head(k_ref, j, 0)[kvh]
            for pair in pairs:
                q2 = q_ref[cur, pair * LANES:(pair + 1) * LANES]
                for h in (0, 1):
                    s_buf[slot, 2 * pair + h] = _nt_dot(q2, ks[h]) + bias

        def softmax(j, slot, pairs=pairs):
            for pair in pairs:
                sink_terms = []
                for h in (0, 1):
                    sink = sink_ref[kvp * 2 * SWA_PAIRS + 2 * pair + h] * LOG2E
                    s = s_buf[slot, 2 * pair + h]
                    m = jnp.maximum(jnp.max(s, axis=-1, keepdims=True), sink)
                    p_buf[slot, 2 * pair + h] = jnp.exp2(s - m).astype(bf16)
                    sink_terms.append(jnp.exp2(sink - m))
                e_buf[slot, pair] = jnp.where(lo, sink_terms[0], sink_terms[1])

        def pv(j, slot, kvh=kvh, pairs=pairs):
            _, cur = _block_rows(j)
            vs = kv_per_query_head(v_ref, j, 1)[kvh]
            for pair in pairs:
                accs = [jnp.dot(p_buf[slot, 2 * pair + h], vs[h], preferred_element_type=f32) for h in (0, 1)]
                num, den = _join_pair(*accs)
                den = den + e_buf[slot, pair]
                o_ref[cur, pair * LANES:(pair + 1) * LANES] = (num / den).astype(bf16)

        _run_pipeline(seq // WINDOW, qk, softmax, pv)


def _swa(proj, sinks, batch, seq):
    m = batch * seq
    qw = SWA_PAIRS * LANES
    nh = 2 * SWA_PAIRS
    return pl.pallas_call(
        _swa_kernel,
        grid=(batch, 2),
        in_specs=[pl.BlockSpec(memory_space=pltpu.SMEM),
                  pl.BlockSpec((seq, qw), lambda b, p: (b, COL_QB // qw + p)),
                  pl.BlockSpec((seq, LANES), lambda b, p: (b, COL_KB // LANES + p)),
                  pl.BlockSpec((seq, LANES), lambda b, p: (b, COL_VB // LANES + p))],
        out_specs=pl.BlockSpec((seq, qw), lambda b, p: (b, p)),
        out_shape=jax.ShapeDtypeStruct((m, D_MODEL), bf16),
        scratch_shapes=[pltpu.VMEM((2, WINDOW, 2 * WINDOW), f32),
                        pltpu.VMEM((2, nh, WINDOW, 2 * WINDOW), f32),
                        pltpu.VMEM((2, nh, WINDOW, 2 * WINDOW), bf16),
                        pltpu.VMEM((2, SWA_PAIRS, WINDOW, LANES), f32)],
        compiler_params=_cparams(2),
        name="swa",
    )(sinks, proj, proj, proj)


def _dil_kernel(q_ref, k_ref, v_ref, o_ref, qf, kf, vf, num1, den1, m1, num2, den2, m2, num3, den3, m3,
                bias_ref, causal_ref, s_buf, p_buf):
    seq = q_ref.shape[0]
    lo = _lane_lo()
    nb2 = seq // (DIL_2 * WINDOW)
    bias_ref[0] = _band_bias(False)
    bias_ref[1] = _band_bias(True)
    causal_ref[...] = _causal_bias()
    qf[...] = q_ref[...].astype(f32)
    kf[...] = k_ref[...].astype(f32)
    vf[...] = v_ref[...].astype(f32)

    def rows2(j):
        r, jb = j // nb2, j % nb2
        cur = r + DIL_2 * WINDOW * jb
        prev = r + DIL_2 * WINDOW * jnp.maximum(jb - 1, 0)
        return pl.ds(prev, WINDOW, stride=DIL_2), pl.ds(cur, WINDOW, stride=DIL_2), jb == 0

    def rows3(j):
        return pl.ds(j, WINDOW, stride=DIL_3)

    def gather_kv(ref, ref_f, j):
        prev1, cur1 = _block_rows(j)
        prev2, cur2, _ = rows2(j)
        return (jnp.concatenate([ref[prev1, :], ref[cur1, :]], axis=0),
                jnp.concatenate([ref_f[prev2, :], ref_f[cur2, :]], axis=0).astype(bf16),
                ref_f[rows3(j), :].astype(bf16))

    def qk(j, slot):
        _, cur1 = _block_rows(j)
        _, cur2, first2 = rows2(j)
        qs = (q_ref[cur1, :], qf[cur2, :].astype(bf16), qf[rows3(j), :].astype(bf16))
        ks = gather_kv(k_ref, kf, j)
        biases = (bias_ref[jnp.where(j == 0, 1, 0)], bias_ref[jnp.where(first2, 1, 0)], causal_ref[...])
        for c in range(DIL_PATTERNS):
            for h, k_h in enumerate(_per_head(ks[c], 0)):
                s = _nt_dot(qs[c], k_h) + biases[c]
                s_buf[slot, 2 * c + h, :, 0:s.shape[1]] = s

    def softmax(j, slot):
        _, cur1 = _block_rows(j)
        _, cur2, _ = rows2(j)
        for c, (m_ref, rows) in enumerate(((m1, cur1), (m2, cur2), (m3, rows3(j)))):
            nk = WINDOW if c == 2 else 2 * WINDOW
            ms = []
            for h in (0, 1):
                s = s_buf[slot, 2 * c + h, :, 0:nk]
                m = jnp.max(s, axis=-1, keepdims=True)
                p_buf[slot, 2 * c + h, :, 0:nk] = jnp.exp2(s - m).astype(bf16)
                ms.append(m)
            m_ref[rows, :] = jnp.where(lo, ms[0], ms[1])

    def pv(j, slot):
        _, cur1 = _block_rows(j)
        _, cur2, _ = rows2(j)
        vs = gather_kv(v_ref, vf, j)
        dests = ((num1, den1, cur1), (num2, den2, cur2), (num3, den3, rows3(j)))
        for c, (num_ref, den_ref, rows) in enumerate(dests):
            nk = WINDOW if c == 2 else 2 * WINDOW
            accs = [jnp.dot(p_buf[slot, 2 * c + h, :, 0:nk], v_h, preferred_element_type=f32)
                    for h, v_h in enumerate(_per_head(vs[c], 1))]
            num, den = _join_pair(*accs)
            num_ref[rows, :] = num
            den_ref[rows, :] = den

    _run_pipeline(seq // WINDOW, qk, softmax, pv)

    ct = 256

    def combine(c, _):
        rows = pl.ds(pl.multiple_of(c * ct, ct), ct)
        ma, mb, mc = m1[rows, :], m2[rows, :], m3[rows, :]
        mx = jnp.maximum(jnp.maximum(ma, mb), mc)
        wa, wb, wc = jnp.exp2(ma - mx), jnp.exp2(mb - mx), jnp.exp2(mc - mx)
        num = wa * num1[rows, :] + wb * num2[rows, :] + wc * num3[rows, :]
        den = wa * den1[rows, :] + wb * den2[rows, :] + wc * den3[rows, :]
        o_ref[rows, :] = (num / den).astype(bf16)
        return 0

    lax.fori_loop(0, seq // ct, combine, 0)


def _dil(proj, batch, seq):
    m = batch * seq
    npair = D_MODEL // LANES
    assert seq == DIL_3 * WINDOW, "pattern 3 is written for one 128-token block per subsequence"
    spec = lambda col: pl.BlockSpec((seq, LANES), lambda b, p: (b, col // LANES + p))
    nh = 2 * DIL_PATTERNS
    return pl.pallas_call(
        _dil_kernel,
        grid=(batch, npair),
        in_specs=[spec(COL_QC), spec(COL_KC), spec(COL_VC)],
        out_specs=pl.BlockSpec((seq, LANES), lambda b, p: (b, p)),
        out_shape=jax.ShapeDtypeStruct((m, D_MODEL), bf16),
        scratch_shapes=[pltpu.VMEM((seq, LANES), f32)] * 12
        + [pltpu.VMEM((2, WINDOW, 2 * WINDOW), f32), pltpu.VMEM((WINDOW, WINDOW), f32),
           pltpu.VMEM((2, nh, WINDOW, 2 * WINDOW), f32), pltpu.VMEM((2, nh, WINDOW, 2 * WINDOW), bf16)],
        compiler_params=_cparams(2),
        name="dil",
    )(proj, proj, proj)


def _layer_norm(z, g, b):
    mu = jnp.mean(z, axis=-1, keepdims=True)
    zc = z - mu
    var = jnp.mean(zc * zc, axis=-1, keepdims=True)
    return zc * lax.rsqrt(var + LN_EPS) * g + b


def _merge_kernel(x_ref, ya_ref, yb_ref, yc_ref, gates_ref, wb_ref, wo_ref, g_ref, b_ref, o_ref):
    merged = None
    for n, y_ref in enumerate((ya_ref, yb_ref, yc_ref)):
        branch = jnp.dot(y_ref[...], wb_ref[n], preferred_element_type=f32)
        gate = jax.nn.sigmoid(gates_ref[:, n * D_MODEL:(n + 1) * D_MODEL].astype(f32))
        term = gate * branch
        merged = term if merged is None else merged + term
    mix = jnp.dot(merged.astype(bf16), wo_ref[...], preferred_element_type=f32)
    o_ref[...] = _layer_norm(ALPHA * x_ref[...] + mix, g_ref[...], b_ref[...])


def _merge(x, ya, yb, yc, proj, wb, wo, g, b):
    m = x.shape[0]
    row = lambda width, col_block: pl.BlockSpec((ROW_TM, width), lambda i: (i, col_block))
    gw = N_BRANCHES * D_MODEL
    return pl.pallas_call(
        _merge_kernel,
        grid=(m // ROW_TM,),
        in_specs=[row(D_MODEL, 0), row(D_MODEL, 0), row(D_MODEL, 0), row(D_MODEL, 0), row(gw, COL_GATES // gw),
                  _resident((N_BRANCHES, D_MODEL, D_MODEL), lambda i: (0, 0, 0)),
                  _resident((D_MODEL, D_MODEL), lambda i: (0, 0)),
                  _resident((1, D_MODEL), lambda i: (0, 0)),
                  _resident((1, D_MODEL), lambda i: (0, 0))],
        out_specs=row(D_MODEL, 0),
        out_shape=jax.ShapeDtypeStruct((m, D_MODEL), f32),
        compiler_params=_cparams(1),
        name="merge",
    )(x, ya, yb, yc, proj, wb, wo, g, b)


def _ffn_kernel(x_ref, win_ref, wout_ref, g_ref, b_ref, o_ref):
    x = x_ref[...]
    xb = x.astype(bf16)
    acc = None
    for c in range(FF_HIDDEN // FFN_CH):
        h1 = jnp.dot(xb, win_ref[:, c * FFN_CH:(c + 1) * FFN_CH], preferred_element_type=f32)
        h3 = jnp.dot(xb, win_ref[:, FF_HIDDEN + c * FFN_CH:FF_HIDDEN + (c + 1) * FFN_CH], preferred_element_type=f32)
        act = (jax.nn.silu(h1) * h3).astype(bf16)
        part = jnp.dot(act, wout_ref[c * FFN_CH:(c + 1) * FFN_CH, :], preferred_element_type=f32)
        acc = part if acc is None else acc + part
    o_ref[...] = _layer_norm(ALPHA * x + acc, g_ref[...], b_ref[...])


def _ffn(x, win, wout, g, b):
    m = x.shape[0]
    return pl.pallas_call(
        _ffn_kernel,
        grid=(m // ROW_TM,),
        in_specs=[pl.BlockSpec((ROW_TM, D_MODEL), lambda i: (i, 0)),
                  _resident((D_MODEL, 2 * FF_HIDDEN), lambda i: (0, 0)),
                  _resident((FF_HIDDEN, D_MODEL), lambda i: (0, 0)),
                  _resident((1, D_MODEL), lambda i: (0, 0)),
                  _resident((1, D_MODEL), lambda i: (0, 0))],
        out_specs=pl.BlockSpec((ROW_TM, D_MODEL), lambda i: (i, 0)),
        out_shape=jax.ShapeDtypeStruct((m, D_MODEL), f32),
        compiler_params=_cparams(1),
        name="ffn",
    )(x, win, wout, g, b)


def _prep_w_in(w_in):
    scale = HEAD_DIM ** -0.5 * LOG2E
    lru_x, lru_g, qb, kb, vb, qc, kc, vc, gates = jnp.split(
        w_in, [1024, 2048, 3072, 3328, 3584, 4608, 5632, 6656], axis=-1)
    w = jnp.concatenate([lru_x, lru_g, qb * scale, qc * scale, kc, vc, gates, kb, vb], axis=-1)
    return w.astype(bf16)


def _block_diag(w):
    per = LRU_CW // HEAD_DIM
    w = w.reshape(DEPTH, LRU_BLOCKS // per, per, HEAD_DIM, HEAD_DIM)
    eye = jnp.eye(per, dtype=w.dtype)
    bd = jnp.einsum("lgaij,ab->lgaibj", w, eye)
    return bd.reshape(DEPTH, LRU_BLOCKS // per, LRU_CW, LRU_CW).astype(bf16)


def kernel(x, w_in, conv_w, conv_b, w_rg, b_rg, w_ig, b_ig, lru_lambda, sinks, w_branch, w_out,
           ln1_g, ln1_b, w_ffn_in, w_ffn_out, ln2_g, ln2_b):
    batch, seq, d = x.shape
    assert d == D_MODEL and w_in.shape == (DEPTH, D_MODEL, IN_WIDTH)
    w_in_b = _prep_w_in(w_in)
    wr_b = _block_diag(w_rg)
    wi_b = _block_diag(w_ig)
    wb_b = w_branch.astype(bf16)
    wo_b = w_out.astype(bf16)
    wfi_b = w_ffn_in.astype(bf16)
    wfo_b = w_ffn_out.astype(bf16)
    row = lambda p, l: p[l].reshape(1, D_MODEL)

    h = x.reshape(batch * seq, D_MODEL)
    for l in range(DEPTH):
        proj = _proj(h, w_in_b[l])
        ya = _lru(proj, conv_w[l], row(conv_b, l), wr_b[l], row(b_rg, l), wi_b[l], row(b_ig, l),
                  row(lru_lambda, l), batch, seq)
        yb = _swa(proj, sinks[l], batch, seq)
        yc = _dil(proj, batch, seq)
        h = _merge(h, ya, yb, yc, proj, wb_b[l], wo_b[l], row(ln1_g, l), row(ln1_b, l))
        h = _ffn(h, wfi_b[l], wfo_b[l], row(ln2_g, l), row(ln2_b, l))
    return h.reshape(batch, seq, D_MODEL)
```

```python
import functools

import jax
import jax.numpy as jnp
from jax import lax
from jax.experimental import pallas as pl
from jax.experimental.pallas import tpu as pltpu

f32 = jnp.float32
bf16 = jnp.bfloat16

D_MODEL = 1024
DEPTH = 4
HEAD_DIM = 64
N_BRANCHES = 3
LRU_BLOCKS = 16
CONV_WIDTH = 4
LRU_C = 8.0
WINDOW = 128
DIL_2, DIL_3 = 4, 16
FF_HIDDEN = 2816
ALPHA = (2.0 * DEPTH) ** 0.25
LN_EPS = 1e-5
NEG_INF = -1e30
LOG2E = 1.4426950408889634

LANES = 128
SUBLANES = 8
VMEM_LIMIT = 56 * 1024 * 1024

IN_WIDTH = 9728
LRU_COLS = 2 * D_MODEL
REST_WIDTH = IN_WIDTH - LRU_COLS
COL_GATES = 0
COL_QB = 3072
COL_QC = 4096
COL_KC = 5120
COL_VC = 6144
COL_KB = 7168
COL_VB = 7424

PROJ_TM = 512
PROJ_TN = 512
LRU_CW = 256
LRU_T = 128
ROW_TM = 512
FFN_CH = 256
SWA_PAIRS = 4
DIL_PATTERNS = 3


def _cparams(n_grid):
    return pltpu.CompilerParams(dimension_semantics=("arbitrary",) * n_grid, vmem_limit_bytes=VMEM_LIMIT)


def _resident(block_shape, index_map):
    return pl.BlockSpec(block_shape, index_map, pipeline_mode=pl.Buffered(1))


def _conv(xe, cw, cb):
    xc = cb + cw[CONV_WIDTH - 1:CONV_WIDTH] * xe[SUBLANES:]
    for s in range(1, CONV_WIDTH):
        xc = xc + cw[CONV_WIDTH - 1 - s:CONV_WIDTH - s] * pltpu.roll(xe, s, 0)[SUBLANES:]
    return xc


def _lru_rows(xc, r_pre, i_pre, gate, carry, br, bi, sp):
    groups = LRU_T // SUBLANES
    row8 = lax.broadcasted_iota(jnp.int32, (groups, SUBLANES, LRU_CW), 1)
    r = jax.nn.sigmoid(r_pre + br)
    i = jax.nn.sigmoid(i_pre + bi)
    log_a = (-LRU_C * r) * sp
    a = jnp.exp(log_a)
    mult = jnp.sqrt(1.0 - a * a)
    b = mult * (i * xc)
    a3 = a.reshape(groups, SUBLANES, LRU_CW)
    b3 = b.reshape(groups, SUBLANES, LRU_CW)
    for d in (1, 2, 4):
        a_s = jnp.where(row8 >= d, pltpu.roll(a3, d, 1), 1.0)
        b_s = jnp.where(row8 >= d, pltpu.roll(b3, d, 1), 0.0)
        b3 = a3 * b_s + b3
        a3 = a3 * a_s
    hs = []
    for g in range(groups):
        h8 = a3[g] * carry + b3[g]
        carry = h8[SUBLANES - 1:SUBLANES, :]
        hs.append(h8)
    return (jnp.concatenate(hs, axis=0) * jax.nn.gelu(gate)).astype(bf16), carry


def _proj_lru_kernel(tiles_per_seq, x_ref, w_ref, cw_ref, cb_ref, wr_ref, br_ref, wi_ref, bi_ref, lam_ref,
                     rest_ref, ya_ref, lx_ref, gt_ref, carry_ref):
    i = pl.program_id(0)
    slot = i % 2
    rslot = 1 - slot
    n_rc = PROJ_TM // LRU_T
    n_cg = D_MODEL // LRU_CW
    n_rest = REST_WIDTH // PROJ_TN
    assert n_rc * n_cg == n_rest + 1

    @pl.when(i == 0)
    def _():
        lx_ref[...] = jnp.zeros(lx_ref.shape, f32)
        gt_ref[...] = jnp.zeros(gt_ref.shape, f32)
        carry_ref[...] = jnp.zeros(carry_ref.shape, f32)

    prev_starts_seq = (i - 1) % tiles_per_seq == 0
    this_starts_seq = i % tiles_per_seq == 0
    xb = x_ref[...].astype(bf16)

    def lru_group(cg):
        lanes = slice(cg * LRU_CW, (cg + 1) * LRU_CW)
        xc = _conv(lx_ref[rslot, :, lanes], cw_ref[:, lanes], cb_ref[:, lanes])
        xcb = xc.astype(bf16)
        r_pre = jnp.dot(xcb, wr_ref[cg], preferred_element_type=f32)
        i_pre = jnp.dot(xcb, wi_ref[cg], preferred_element_type=f32)
        sp = jax.nn.softplus(-lam_ref[:, lanes])
        state = {"carry": jnp.where(prev_starts_seq, 0.0, carry_ref[:, lanes])}

        def finish(rc):
            rows = slice(rc * LRU_T, (rc + 1) * LRU_T)
            y, state["carry"] = _lru_rows(xc[rows], r_pre[rows], i_pre[rows], gt_ref[rslot, rows, lanes],
                                          state["carry"], br_ref[:, lanes], bi_ref[:, lanes], sp)
            ya_ref[rows, lanes] = y
            if rc == n_rc - 1:
                carry_ref[:, lanes] = state["carry"]

        return finish

    finish = lru_group(0)
    lx_ref[slot, SUBLANES:, :] = jnp.dot(xb, w_ref[:, 0:D_MODEL], preferred_element_type=f32)
    gt_ref[slot] = jnp.dot(xb, w_ref[:, D_MODEL:LRU_COLS], preferred_element_type=f32)
    tail = lx_ref[rslot, PROJ_TM:PROJ_TM + SUBLANES, :]
    lx_ref[slot, 0:SUBLANES, :] = jnp.where(this_starts_seq, 0.0, tail)

    for c in range(n_rc * n_cg):
        cg, rc = c // n_rc, c % n_rc
        if rc == 0 and cg > 0:
            finish = lru_group(cg)
        if c < n_rest:
            cols = slice(c * PROJ_TN, (c + 1) * PROJ_TN)
            acc = jnp.dot(xb, w_ref[:, LRU_COLS + c * PROJ_TN:LRU_COLS + (c + 1) * PROJ_TN],
                          preferred_element_type=f32)
            rest_ref[:, cols] = acc.astype(bf16)
        finish(rc)


def _proj_lru(x, w, cw, cb, wr, br, wi, bi, lam, seq):
    m = x.shape[0]
    nt = m // PROJ_TM
    last = nt - 1
    vec = lambda rows: _resident((rows, D_MODEL), lambda i: (0, 0))
    mat = _resident((D_MODEL // LRU_CW, LRU_CW, LRU_CW), lambda i: (0, 0, 0))
    return pl.pallas_call(
        functools.partial(_proj_lru_kernel, seq // PROJ_TM),
        grid=(nt + 1,),
        in_specs=[pl.BlockSpec((PROJ_TM, D_MODEL), lambda i: (jnp.minimum(i, last), 0)),
                  _resident((D_MODEL, IN_WIDTH), lambda i: (0, 0)),
                  vec(CONV_WIDTH), vec(1), mat, vec(1), mat, vec(1), vec(1)],
        out_specs=[pl.BlockSpec((PROJ_TM, REST_WIDTH), lambda i: (jnp.minimum(i, last), 0)),
                   pl.BlockSpec((PROJ_TM, D_MODEL), lambda i: (jnp.maximum(i - 1, 0), 0))],
        out_shape=[jax.ShapeDtypeStruct((m, REST_WIDTH), bf16), jax.ShapeDtypeStruct((m, D_MODEL), bf16)],
        scratch_shapes=[pltpu.VMEM((2, PROJ_TM + SUBLANES, D_MODEL), f32),
                        pltpu.VMEM((2, PROJ_TM, D_MODEL), f32),
                        pltpu.VMEM((1, D_MODEL), f32)],
        compiler_params=_cparams(1),
        name="proj_lru",
    )(x, w, cw, cb, wr, br, wi, bi, lam)


def _band_bias(first_block):
    qi = lax.broadcasted_iota(jnp.int32, (WINDOW, 2 * WINDOW), 0)
    kj = lax.broadcasted_iota(jnp.int32, (WINDOW, 2 * WINDOW), 1)
    rel = qi + WINDOW - kj
    ok = jnp.logical_and(rel >= 0, rel <= WINDOW)
    if first_block:
        ok = jnp.logical_and(ok, kj >= WINDOW)
    return jnp.where(ok, 0.0, NEG_INF).astype(f32)


def _causal_bias():
    qi = lax.broadcasted_iota(jnp.int32, (WINDOW, WINDOW), 0)
    kj = lax.broadcasted_iota(jnp.int32, (WINDOW, WINDOW), 1)
    return jnp.where(qi >= kj, 0.0, NEG_INF).astype(f32)


def _nt_dot(a, b):
    return lax.dot_general(a, b, (((1,), (1,)), ((), ())), preferred_element_type=f32)


def _lane_lo():
    return lax.broadcasted_iota(jnp.int32, (1, LANES), 1) < HEAD_DIM


def _per_head(x, fill):
    lo = _lane_lo()
    f = jnp.full_like(x, fill)
    return jnp.where(lo, x, f), jnp.where(lo, f, x)


def _join_pair(a0, a1):
    lo = _lane_lo()
    return jnp.where(lo, a0, a1), pltpu.roll(jnp.where(lo, a1, a0), HEAD_DIM, 1)


def _run_pipeline(nblk, qk, softmax, pv):
    assert nblk % 2 == 0 and nblk >= 4
    qk(0, 0)
    qk(1, 1)
    softmax(0, 0)

    def two_steps(i, _):
        for slot in (0, 1):
            t = 2 * i + slot
            pv(t - 2, slot)
            qk(t, slot)
            softmax(t - 1, 1 - slot)
        return 0

    lax.fori_loop(1, nblk // 2, two_steps, 0)
    pv(nblk - 2, 0)
    softmax(nblk - 1, 1)
    pv(nblk - 1, 1)


def _block_rows(j):
    if isinstance(j, int):
        return pl.ds(max(j - 1, 0) * WINDOW, WINDOW), pl.ds(j * WINDOW, WINDOW)
    cur = pl.multiple_of(j * WINDOW, WINDOW)
    prev = pl.multiple_of(jnp.maximum(j - 1, 0) * WINDOW, WINDOW)
    return pl.ds(prev, WINDOW), pl.ds(cur, WINDOW)


def _swa_kernel(sink_ref, q_ref, k_ref, v_ref, o_ref, bias_ref, s_buf, p_buf, e_buf):
    seq = q_ref.shape[0]
    kvp = pl.program_id(1)
    lo = _lane_lo()
    bias_ref[0] = _band_bias(False)
    bias_ref[1] = _band_bias(True)

    def kv_per_query_head(ref, j, fill):
        prev, cur = _block_rows(j)
        xx = jnp.concatenate([ref[prev, :], ref[cur, :]], axis=0)
        xr = pltpu.roll(xx, HEAD_DIM, 1)
        f = jnp.full_like(xx, fill)
        return ((jnp.where(lo, xx, f), jnp.where(lo, f, xr)),
                (jnp.where(lo, xr, f), jnp.where(lo, f, xx)))

    for kvh in (0, 1):
        pairs = (2 * kvh, 2 * kvh + 1)

        def qk(j, slot, kvh=kvh, pairs=pairs):
            _, cur = _block_rows(j)
            bias = bias_ref[jnp.where(j == 0, 1, 0)]
            ks = kv_per_query_head(k_ref, j, 0)[kvh]
            for pair in pairs:
                q2 = q_ref[cur, pair * LANES:(pair + 1) * LANES]
                for h in (0, 1):
                    s_buf[slot, 2 * pair + h] = _nt_dot(q2, ks[h]) + bias

        def softmax(j, slot, pairs=pairs):
            for pair in pairs:
                sink_terms = []
                for h in (0, 1):
                    sink = sink_ref[kvp * 2 * SWA_PAIRS + 2 * pair + h] * LOG2E
                    s = s_buf[slot, 2 * pair + h]
                    m = jnp.maximum(jnp.max(s, axis=-1, keepdims=True), sink)
                    p_buf[slot, 2 * pair + h] = jnp.exp2(s - m).astype(bf16)
                    sink_terms.append(jnp.exp2(sink - m))
                e_buf[slot, pair] = jnp.where(lo, sink_terms[0], sink_terms[1])

        def pv(j, slot, kvh=kvh, pairs=pairs):
            _, cur = _block_rows(j)
            vs = kv_per_query_head(v_ref, j, 1)[kvh]
            for pair in pairs:
                accs = [jnp.dot(p_buf[slot, 2 * pair + h], vs[h], preferred_element_type=f32) for h in (0, 1)]
                num, den = _join_pair(*accs)
                den = den + e_buf[slot, pair]
                o_ref[cur, pair * LANES:(pair + 1) * LANES] = (num / den).astype(bf16)

        _run_pipeline(seq // WINDOW, qk, softmax, pv)


def _swa(proj, sinks, batch, seq):
    m = batch * seq
    qw = SWA_PAIRS * LANES
    nh = 2 * SWA_PAIRS
    return pl.pallas_call(
        _swa_kernel,
        grid=(batch, 2),
        in_specs=[pl.BlockSpec(memory_space=pltpu.SMEM),
                  pl.BlockSpec((seq, qw), lambda b, p: (b, COL_QB // qw + p)),
                  pl.BlockSpec((seq, LANES), lambda b, p: (b, COL_KB // LANES + p)),
                  pl.BlockSpec((seq, LANES), lambda b, p: (b, COL_VB // LANES + p))],
        out_specs=pl.BlockSpec((seq, qw), lambda b, p: (b, p)),
        out_shape=jax.ShapeDtypeStruct((m, D_MODEL), bf16),
        scratch_shapes=[pltpu.VMEM((2, WINDOW, 2 * WINDOW), f32),
                        pltpu.VMEM((2, nh, WINDOW, 2 * WINDOW), f32),
                        pltpu.VMEM((2, nh, WINDOW, 2 * WINDOW), bf16),
                        pltpu.VMEM((2, SWA_PAIRS, WINDOW, LANES), f32)],
        compiler_params=_cparams(2),
        name="swa",
    )(sinks, proj, proj, proj)


def _dil_kernel(q_ref, k_ref, v_ref, o_ref, qf, kf, vf, num1, den1, m1, num2, den2, m2, num3, den3, m3,
                bias_ref, causal_ref, s_buf, p_buf):
    seq = q_ref.shape[0]
    lo = _lane_lo()
    nb2 = seq // (DIL_2 * WINDOW)
    bias_ref[0] = _band_bias(False)
    bias_ref[1] = _band_bias(True)
    causal_ref[...] = _causal_bias()
    qf[...] = q_ref[...].astype(f32)
    kf[...] = k_ref[...].astype(f32)
    vf[...] = v_ref[...].astype(f32)

    def rows2(j):
        r, jb = j // nb2, j % nb2
        cur = r + DIL_2 * WINDOW * jb
        prev = r + DIL_2 * WINDOW * jnp.maximum(jb - 1, 0)
        return pl.ds(prev, WINDOW, stride=DIL_2), pl.ds(cur, WINDOW, stride=DIL_2), jb == 0

    def rows3(j):
        return pl.ds(j, WINDOW, stride=DIL_3)

    def gather_kv(ref, ref_f, j):
        prev1, cur1 = _block_rows(j)
        prev2, cur2, _ = rows2(j)
        return (jnp.concatenate([ref[prev1, :], ref[cur1, :]], axis=0),
                jnp.concatenate([ref_f[prev2, :], ref_f[cur2, :]], axis=0).astype(bf16),
                ref_f[rows3(j), :].astype(bf16))

    def qk(j, slot):
        _, cur1 = _block_rows(j)
        _, cur2, first2 = rows2(j)
        qs = (q_ref[cur1, :], qf[cur2, :].astype(bf16), qf[rows3(j), :].astype(bf16))
        ks = gather_kv(k_ref, kf, j)
        biases = (bias_ref[jnp.where(j == 0, 1, 0)], bias_ref[jnp.where(first2, 1, 0)], causal_ref[...])
        for c in range(DIL_PATTERNS):
            for h, k_h in enumerate(_per_head(ks[c], 0)):
                s = _nt_dot(qs[c], k_h) + biases[c]
                s_buf[slot, 2 * c + h, :, 0:s.shape[1]] = s

    def softmax(j, slot):
        _, cur1 = _block_rows(j)
        _, cur2, _ = rows2(j)
        for c, (m_ref, rows) in enumerate(((m1, cur1), (m2, cur2), (m3, rows3(j)))):
            nk = WINDOW if c == 2 else 2 * WINDOW
            ms = []
            for h in (0, 1):
                s = s_buf[slot, 2 * c + h, :, 0:nk]
                m = jnp.max(s, axis=-1, keepdims=True)
                p_buf[slot, 2 * c + h, :, 0:nk] = jnp.exp2(s - m).astype(bf16)
                ms.append(m)
            m_ref[rows, :] = jnp.where(lo, ms[0], ms[1])

    def pv(j, slot):
        _, cur1 = _block_rows(j)
        _, cur2, _ = rows2(j)
        vs = gather_kv(v_ref, vf, j)
        dests = ((num1, den1, cur1), (num2, den2, cur2), (num3, den3, rows3(j)))
        for c, (num_ref, den_ref, rows) in enumerate(dests):
            nk = WINDOW if c == 2 else 2 * WINDOW
            accs = [jnp.dot(p_buf[slot, 2 * c + h, :, 0:nk], v_h, preferred_element_type=f32)
                    for h, v_h in enumerate(_per_head(vs[c], 1))]
            num, den = _join_pair(*accs)
            num_ref[rows, :] = num
            den_ref[rows, :] = den

    _run_pipeline(seq // WINDOW, qk, softmax, pv)

    ct = 256

    def combine(c, _):
        rows = pl.ds(pl.multiple_of(c * ct, ct), ct)
        ma, mb, mc = m1[rows, :], m2[rows, :], m3[rows, :]
        mx = jnp.maximum(jnp.maximum(ma, mb), mc)
        wa, wb, wc = jnp.exp2(ma - mx), jnp.exp2(mb - mx), jnp.exp2(mc - mx)
        num = wa * num1[rows, :] + wb * num2[rows, :] + wc * num3[rows, :]
        den = wa * den1[rows, :] + wb * den2[rows, :] + wc * den3[rows, :]
        o_ref[rows, :] = (num / den).astype(bf16)
        return 0

    lax.fori_loop(0, seq // ct, combine, 0)


def _dil(proj, batch, seq):
    m = batch * seq
    npair = D_MODEL // LANES
    assert seq == DIL_3 * WINDOW, "pattern 3 is written for one 128-token block per subsequence"
    spec = lambda col: pl.BlockSpec((seq, LANES), lambda b, p: (b, col // LANES + p))
    nh = 2 * DIL_PATTERNS
    return pl.pallas_call(
        _dil_kernel,
        grid=(batch, npair),
        in_specs=[spec(COL_QC), spec(COL_KC), spec(COL_VC)],
        out_specs=pl.BlockSpec((seq, LANES), lambda b, p: (b, p)),
        out_shape=jax.ShapeDtypeStruct((m, D_MODEL), bf16),
        scratch_shapes=[pltpu.VMEM((seq, LANES), f32)] * 12
        + [pltpu.VMEM((2, WINDOW, 2 * WINDOW), f32), pltpu.VMEM((WINDOW, WINDOW), f32),
           pltpu.VMEM((2, nh, WINDOW, 2 * WINDOW), f32), pltpu.VMEM((2, nh, WINDOW, 2 * WINDOW), bf16)],
        compiler_params=_cparams(2),
        name="dil",
    )(proj, proj, proj)


def _layer_norm(z, g, b):
    mu = jnp.mean(z, axis=-1, keepdims=True)
    zc = z - mu
    var = jnp.mean(zc * zc, axis=-1, keepdims=True)
    return zc * lax.rsqrt(var + LN_EPS) * g + b


def _merge_kernel(x_ref, ya_ref, yb_ref, yc_ref, gates_ref, wb_ref, wo_ref, g_ref, b_ref, o_ref):
    merged = None
    for n, y_ref in enumerate((ya_ref, yb_ref, yc_ref)):
        branch = jnp.dot(y_ref[...], wb_ref[n], preferred_element_type=f32)
        gate = jax.nn.sigmoid(gates_ref[:, n * D_MODEL:(n + 1) * D_MODEL].astype(f32))
        term = gate * branch
        merged = term if merged is None else merged + term
    mix = jnp.dot(merged.astype(bf16), wo_ref[...], preferred_element_type=f32)
    o_ref[...] = _layer_norm(ALPHA * x_ref[...] + mix, g_ref[...], b_ref[...])


def _merge(x, ya, yb, yc, proj, wb, wo, g, b):
    m = x.shape[0]
    row = lambda width, col_block: pl.BlockSpec((ROW_TM, width), lambda i: (i, col_block))
    gw = N_BRANCHES * D_MODEL
    return pl.pallas_call(
        _merge_kernel,
        grid=(m // ROW_TM,),
        in_specs=[row(D_MODEL, 0), row(D_MODEL, 0), row(D_MODEL, 0), row(D_MODEL, 0), row(gw, COL_GATES // gw),
                  _resident((N_BRANCHES, D_MODEL, D_MODEL), lambda i: (0, 0, 0)),
                  _resident((D_MODEL, D_MODEL), lambda i: (0, 0)),
                  _resident((1, D_MODEL), lambda i: (0, 0)),
                  _resident((1, D_MODEL), lambda i: (0, 0))],
        out_specs=row(D_MODEL, 0),
        out_shape=jax.ShapeDtypeStruct((m, D_MODEL), f32),
        compiler_params=_cparams(1),
        name="merge",
    )(x, ya, yb, yc, proj, wb, wo, g, b)


def _ffn_kernel(x_ref, win_ref, wout_ref, g_ref, b_ref, o_ref):
    x = x_ref[...]
    xb = x.astype(bf16)
    acc = None
    for c in range(FF_HIDDEN // FFN_CH):
        h1 = jnp.dot(xb, win_ref[:, c * FFN_CH:(c + 1) * FFN_CH], preferred_element_type=f32)
        h3 = jnp.dot(xb, win_ref[:, FF_HIDDEN + c * FFN_CH:FF_HIDDEN + (c + 1) * FFN_CH], preferred_element_type=f32)
        act = (jax.nn.silu(h1) * h3).astype(bf16)
        part = jnp.dot(act, wout_ref[c * FFN_CH:(c + 1) * FFN_CH, :], preferred_element_type=f32)
        acc = part if acc is None else acc + part
    o_ref[...] = _layer_norm(ALPHA * x + acc, g_ref[...], b_ref[...])


def _ffn(x, win, wout, g, b):
    m = x.shape[0]
    return pl.pallas_call(
        _ffn_kernel,
        grid=(m // ROW_TM,),
        in_specs=[pl.BlockSpec((ROW_TM, D_MODEL), lambda i: (i, 0)),
                  _resident((D_MODEL, 2 * FF_HIDDEN), lambda i: (0, 0)),
                  _resident((FF_HIDDEN, D_MODEL), lambda i: (0, 0)),
                  _resident((1, D_MODEL), lambda i: (0, 0)),
                  _resident((1, D_MODEL), lambda i: (0, 0))],
        out_specs=pl.BlockSpec((ROW_TM, D_MODEL), lambda i: (i, 0)),
        out_shape=jax.ShapeDtypeStruct((m, D_MODEL), f32),
        compiler_params=_cparams(1),
        name="ffn",
    )(x, win, wout, g, b)


def _prep_w_in(w_in):
    scale = HEAD_DIM ** -0.5 * LOG2E
    lru_x, lru_g, qb, kb, vb, qc, kc, vc, gates = jnp.split(
        w_in, [1024, 2048, 3072, 3328, 3584, 4608, 5632, 6656], axis=-1)
    w = jnp.concatenate([lru_x, lru_g, gates, qb * scale, qc * scale, kc, vc, kb, vb], axis=-1)
    return w.astype(bf16)


def _block_diag(w):
    per = LRU_CW // HEAD_DIM
    w = w.reshape(DEPTH, LRU_BLOCKS // per, per, HEAD_DIM, HEAD_DIM)
    eye = jnp.eye(per, dtype=w.dtype)
    bd = jnp.einsum("lgaij,ab->lgaibj", w, eye)
    return bd.reshape(DEPTH, LRU_BLOCKS // per, LRU_CW, LRU_CW).astype(bf16)


def kernel(x, w_in, conv_w, conv_b, w_rg, b_rg, w_ig, b_ig, lru_lambda, sinks, w_branch, w_out,
           ln1_g, ln1_b, w_ffn_in, w_ffn_out, ln2_g, ln2_b):
    batch, seq, d = x.shape
    assert d == D_MODEL and w_in.shape == (DEPTH, D_MODEL, IN_WIDTH)
    w_in_b = _prep_w_in(w_in)
    wr_b = _block_diag(w_rg)
    wi_b = _block_diag(w_ig)
    wb_b = w_branch.astype(bf16)
    wo_b = w_out.astype(bf16)
    wfi_b = w_ffn_in.astype(bf16)
    wfo_b = w_ffn_out.astype(bf16)
    row = lambda p, l: p[l].reshape(1, D_MODEL)

    h = x.reshape(batch * seq, D_MODEL)
    for l in range(DEPTH):
        proj, ya = _proj_lru(h, w_in_b[l], conv_w[l], row(conv_b, l), wr_b[l], row(b_rg, l), wi_b[l],
                             row(b_ig, l), row(lru_lambda, l), seq)
        yb = _swa(proj, sinks[l], batch, seq)
        yc = _dil(proj, batch, seq)
        h = _merge(h, ya, yb, yc, proj, wb_b[l], wo_b[l], row(ln1_g, l), row(ln1_b, l))
        h = _ffn(h, wfi_b[l], wfo_b[l], row(ln2_g, l), row(ln2_b, l))
    return h.reshape(batch, seq, D_MODEL)
```

```python
import functools

import jax
import jax.numpy as jnp
from jax import lax
from jax.experimental import pallas as pl
from jax.experimental.pallas import tpu as pltpu

f32 = jnp.float32
bf16 = jnp.bfloat16

D_MODEL = 1024
DEPTH = 4
HEAD_DIM = 64
N_BRANCHES = 3
LRU_BLOCKS = 16
CONV_WIDTH = 4
LRU_C = 8.0
WINDOW = 128
DIL_2, DIL_3 = 4, 16
FF_HIDDEN = 2816
ALPHA = (2.0 * DEPTH) ** 0.25
LN_EPS = 1e-5
NEG_INF = -1e30
LOG2E = 1.4426950408889634

LANES = 128
SUBLANES = 8
VMEM_LIMIT = 56 * 1024 * 1024

IN_WIDTH = 9728
LRU_COLS = 2 * D_MODEL
REST_WIDTH = IN_WIDTH - LRU_COLS
COL_GATES = 0
COL_QB = 3072
COL_QC = 4096
COL_KC = 5120
COL_VC = 6144
COL_KB = 7168
COL_VB = 7424

PROJ_TM = 512
PROJ_TN = 512
LRU_CW = 256
LRU_T = 128
ROW_TM = 1024
ROW_SUB = 512
FFN_CH = 256
SWA_PAIRS = 4
DIL_PATTERNS = 3
SWA_STEPS = 2
DIL_STEPS = 4


def _cparams(n_grid):
    return pltpu.CompilerParams(dimension_semantics=("arbitrary",) * n_grid, vmem_limit_bytes=VMEM_LIMIT)


def _resident(block_shape, index_map):
    return pl.BlockSpec(block_shape, index_map, pipeline_mode=pl.Buffered(1))


def _conv(xe, cw, cb):
    xc = cb + cw[CONV_WIDTH - 1:CONV_WIDTH] * xe[SUBLANES:]
    for s in range(1, CONV_WIDTH):
        xc = xc + cw[CONV_WIDTH - 1 - s:CONV_WIDTH - s] * pltpu.roll(xe, s, 0)[SUBLANES:]
    return xc


def _lru_rows(xc, r_pre, i_pre, gate, carry, br, bi, sp):
    groups = LRU_T // SUBLANES
    row8 = lax.broadcasted_iota(jnp.int32, (groups, SUBLANES, LRU_CW), 1)
    r = jax.nn.sigmoid(r_pre + br)
    i = jax.nn.sigmoid(i_pre + bi)
    log_a = (-LRU_C * r) * sp
    a = jnp.exp(log_a)
    mult = jnp.sqrt(1.0 - a * a)
    b = mult * (i * xc)
    a3 = a.reshape(groups, SUBLANES, LRU_CW)
    b3 = b.reshape(groups, SUBLANES, LRU_CW)
    for d in (1, 2, 4):
        a_s = jnp.where(row8 >= d, pltpu.roll(a3, d, 1), 1.0)
        b_s = jnp.where(row8 >= d, pltpu.roll(b3, d, 1), 0.0)
        b3 = a3 * b_s + b3
        a3 = a3 * a_s
    hs = []
    for g in range(groups):
        h8 = a3[g] * carry + b3[g]
        carry = h8[SUBLANES - 1:SUBLANES, :]
        hs.append(h8)
    return (jnp.concatenate(hs, axis=0) * jax.nn.gelu(gate)).astype(bf16), carry


def _proj_lru_kernel(tiles_per_seq, x_ref, w_ref, cw_ref, cb_ref, wr_ref, br_ref, wi_ref, bi_ref, lam_ref,
                     rest_ref, ya_ref, lx_ref, gt_ref, carry_ref):
    i = pl.program_id(0)
    slot = i % 2
    rslot = 1 - slot
    n_rc = PROJ_TM // LRU_T
    n_cg = D_MODEL // LRU_CW
    n_rest = REST_WIDTH // PROJ_TN
    assert n_rc * n_cg == n_rest + 1

    @pl.when(i == 0)
    def _():
        lx_ref[...] = jnp.zeros(lx_ref.shape, f32)
        gt_ref[...] = jnp.zeros(gt_ref.shape, f32)
        carry_ref[...] = jnp.zeros(carry_ref.shape, f32)

    prev_starts_seq = (i - 1) % tiles_per_seq == 0
    this_starts_seq = i % tiles_per_seq == 0
    xb = x_ref[...].astype(bf16)

    def lru_group(cg):
        lanes = slice(cg * LRU_CW, (cg + 1) * LRU_CW)
        xc = _conv(lx_ref[rslot, :, lanes], cw_ref[:, lanes], cb_ref[:, lanes])
        xcb = xc.astype(bf16)
        r_pre = jnp.dot(xcb, wr_ref[cg], preferred_element_type=f32)
        i_pre = jnp.dot(xcb, wi_ref[cg], preferred_element_type=f32)
        sp = jax.nn.softplus(-lam_ref[:, lanes])
        state = {"carry": jnp.where(prev_starts_seq, 0.0, carry_ref[:, lanes])}

        def finish(rc):
            rows = slice(rc * LRU_T, (rc + 1) * LRU_T)
            y, state["carry"] = _lru_rows(xc[rows], r_pre[rows], i_pre[rows], gt_ref[rslot, rows, lanes],
                                          state["carry"], br_ref[:, lanes], bi_ref[:, lanes], sp)
            ya_ref[rows, lanes] = y
            if rc == n_rc - 1:
                carry_ref[:, lanes] = state["carry"]

        return finish

    finish = lru_group(0)
    lx_ref[slot, SUBLANES:, :] = jnp.dot(xb, w_ref[:, 0:D_MODEL], preferred_element_type=f32)
    gt_ref[slot] = jnp.dot(xb, w_ref[:, D_MODEL:LRU_COLS], preferred_element_type=f32)
    tail = lx_ref[rslot, PROJ_TM:PROJ_TM + SUBLANES, :]
    lx_ref[slot, 0:SUBLANES, :] = jnp.where(this_starts_seq, 0.0, tail)

    for c in range(n_rc * n_cg):
        cg, rc = c // n_rc, c % n_rc
        if rc == 0 and cg > 0:
            finish = lru_group(cg)
        if c < n_rest:
            cols = slice(c * PROJ_TN, (c + 1) * PROJ_TN)
            acc = jnp.dot(xb, w_ref[:, LRU_COLS + c * PROJ_TN:LRU_COLS + (c + 1) * PROJ_TN],
                          preferred_element_type=f32)
            rest_ref[:, cols] = acc.astype(bf16)
        finish(rc)


def _proj_lru(x, w, cw, cb, wr, br, wi, bi, lam, seq):
    m = x.shape[0]
    nt = m // PROJ_TM
    last = nt - 1
    vec = lambda rows: _resident((rows, D_MODEL), lambda i: (0, 0))
    mat = _resident((D_MODEL // LRU_CW, LRU_CW, LRU_CW), lambda i: (0, 0, 0))
    return pl.pallas_call(
        functools.partial(_proj_lru_kernel, seq // PROJ_TM),
        grid=(nt + 1,),
        in_specs=[pl.BlockSpec((PROJ_TM, D_MODEL), lambda i: (jnp.minimum(i, last), 0)),
                  _resident((D_MODEL, IN_WIDTH), lambda i: (0, 0)),
                  vec(CONV_WIDTH), vec(1), mat, vec(1), mat, vec(1), vec(1)],
        out_specs=[pl.BlockSpec((PROJ_TM, REST_WIDTH), lambda i: (jnp.minimum(i, last), 0)),
                   pl.BlockSpec((PROJ_TM, D_MODEL), lambda i: (jnp.maximum(i - 1, 0), 0))],
        out_shape=[jax.ShapeDtypeStruct((m, REST_WIDTH), bf16), jax.ShapeDtypeStruct((m, D_MODEL), bf16)],
        scratch_shapes=[pltpu.VMEM((2, PROJ_TM + SUBLANES, D_MODEL), f32),
                        pltpu.VMEM((2, PROJ_TM, D_MODEL), f32),
                        pltpu.VMEM((1, D_MODEL), f32)],
        compiler_params=_cparams(1),
        name="proj_lru",
    )(x, w, cw, cb, wr, br, wi, bi, lam)


def _band_bias(first_block):
    qi = lax.broadcasted_iota(jnp.int32, (WINDOW, 2 * WINDOW), 0)
    kj = lax.broadcasted_iota(jnp.int32, (WINDOW, 2 * WINDOW), 1)
    rel = qi + WINDOW - kj
    ok = jnp.logical_and(rel >= 0, rel <= WINDOW)
    if first_block:
        ok = jnp.logical_and(ok, kj >= WINDOW)
    return jnp.where(ok, 0.0, NEG_INF).astype(f32)


def _causal_bias():
    qi = lax.broadcasted_iota(jnp.int32, (WINDOW, WINDOW), 0)
    kj = lax.broadcasted_iota(jnp.int32, (WINDOW, WINDOW), 1)
    return jnp.where(qi >= kj, 0.0, NEG_INF).astype(f32)


def _nt_dot(a, b):
    return lax.dot_general(a, b, (((1,), (1,)), ((), ())), preferred_element_type=f32)


def _lane_lo():
    return lax.broadcasted_iota(jnp.int32, (1, LANES), 1) < HEAD_DIM


def _per_head(x, fill):
    lo = _lane_lo()
    f = jnp.full_like(x, fill)
    return jnp.where(lo, x, f), jnp.where(lo, f, x)


def _join_pair(a0, a1):
    lo = _lane_lo()
    return jnp.where(lo, a0, a1), pltpu.roll(jnp.where(lo, a1, a0), HEAD_DIM, 1)


def _run_pipeline(nblk, qk, softmax, pv, steps_per_trip):
    assert steps_per_trip % 2 == 0 and nblk % steps_per_trip == 0 and nblk >= 2 * steps_per_trip

    def step(t, slot):
        pv(t - 2, slot)
        qk(t, slot)
        softmax(t - 1, 1 - slot)

    qk(0, 0)
    qk(1, 1)
    softmax(0, 0)
    for t in range(2, steps_per_trip):
        step(t, t % 2)

    def trip(i, _):
        for u in range(steps_per_trip):
            step(steps_per_trip * i + u, u % 2)
        return 0

    lax.fori_loop(1, nblk // steps_per_trip, trip, 0)
    pv(nblk - 2, 0)
    softmax(nblk - 1, 1)
    pv(nblk - 1, 1)


def _block_rows(j):
    if isinstance(j, int):
        return pl.ds(max(j - 1, 0) * WINDOW, WINDOW), pl.ds(j * WINDOW, WINDOW)
    cur = pl.multiple_of(j * WINDOW, WINDOW)
    prev = pl.multiple_of(jnp.maximum(j - 1, 0) * WINDOW, WINDOW)
    return pl.ds(prev, WINDOW), pl.ds(cur, WINDOW)


def _swa_kernel(sink_ref, q_ref, k_ref, v_ref, o_ref, bias_ref, s_buf, p_buf, e_buf):
    seq = q_ref.shape[0]
    kvp = pl.program_id(1)
    lo = _lane_lo()

    @pl.when(jnp.logical_and(pl.program_id(0) == 0, kvp == 0))
    def _():
        bias_ref[0] = _band_bias(False)
        bias_ref[1] = _band_bias(True)

    def kv_per_query_head(ref, j, fill):
        prev, cur = _block_rows(j)
        xx = jnp.concatenate([ref[prev, :], ref[cur, :]], axis=0)
        xr = pltpu.roll(xx, HEAD_DIM, 1)
        f = jnp.full_like(xx, fill)
        return ((jnp.where(lo, xx, f), jnp.where(lo, f, xr)),
                (jnp.where(lo, xr, f), jnp.where(lo, f, xx)))

    for kvh in (0, 1):
        pairs = (2 * kvh, 2 * kvh + 1)

        def qk(j, slot, kvh=kvh, pairs=pairs):
            _, cur = _block_rows(j)
            bias = bias_ref[jnp.where(j == 0, 1, 0)]
            ks = kv_per_query_head(k_ref, j, 0)[kvh]
            for pair in pairs:
                q2 = q_ref[cur, pair * LANES:(pair + 1) * LANES]
                for h in (0, 1):
                    s_buf[slot, 2 * pair + h] = _nt_dot(q2, ks[h]) + bias

        def softmax(j, slot, pairs=pairs):
            for pair in pairs:
                sink_terms = []
                for h in (0, 1):
                    sink = sink_ref[kvp * 2 * SWA_PAIRS + 2 * pair + h] * LOG2E
                    s = s_buf[slot, 2 * pair + h]
                    m = jnp.maximum(jnp.max(s, axis=-1, keepdims=True), sink)
                    p_buf[slot, 2 * pair + h] = jnp.exp2(s - m).astype(bf16)
                    sink_terms.append(jnp.exp2(sink - m))
                e_buf[slot, pair] = jnp.where(lo, sink_terms[0], sink_terms[1])

        def pv(j, slot, kvh=kvh, pairs=pairs):
            _, cur = _block_rows(j)
            vs = kv_per_query_head(v_ref, j, 1)[kvh]
            for pair in pairs:
                accs = [jnp.dot(p_buf[slot, 2 * pair + h], vs[h], preferred_element_type=f32) for h in (0, 1)]
                num, den = _join_pair(*accs)
                den = den + e_buf[slot, pair]
                o_ref[cur, pair * LANES:(pair + 1) * LANES] = (num / den).astype(bf16)

        _run_pipeline(seq // WINDOW, qk, softmax, pv, SWA_STEPS)


def _swa(proj, sinks, batch, seq):
    m = batch * seq
    qw = SWA_PAIRS * LANES
    nh = 2 * SWA_PAIRS
    return pl.pallas_call(
        _swa_kernel,
        grid=(batch, 2),
        in_specs=[pl.BlockSpec(memory_space=pltpu.SMEM),
                  pl.BlockSpec((seq, qw), lambda b, p: (b, COL_QB // qw + p)),
                  pl.BlockSpec((seq, LANES), lambda b, p: (b, COL_KB // LANES + p)),
                  pl.BlockSpec((seq, LANES), lambda b, p: (b, COL_VB // LANES + p))],
        out_specs=pl.BlockSpec((seq, qw), lambda b, p: (b, p)),
        out_shape=jax.ShapeDtypeStruct((m, D_MODEL), bf16),
        scratch_shapes=[pltpu.VMEM((2, WINDOW, 2 * WINDOW), f32),
                        pltpu.VMEM((2, nh, WINDOW, 2 * WINDOW), f32),
                        pltpu.VMEM((2, nh, WINDOW, 2 * WINDOW), bf16),
                        pltpu.VMEM((2, SWA_PAIRS, WINDOW, LANES), f32)],
        compiler_params=_cparams(2),
        name="swa",
    )(sinks, proj, proj, proj)


def _dil_kernel(q_ref, k_ref, v_ref, o_ref, qf, kf, vf, num1, den1, m1, num2, den2, m2, num3, den3, m3,
                bias_ref, causal_ref, s_buf, p_buf):
    seq = q_ref.shape[0]
    lo = _lane_lo()
    nb2 = seq // (DIL_2 * WINDOW)

    @pl.when(jnp.logical_and(pl.program_id(0) == 0, pl.program_id(1) == 0))
    def _():
        bias_ref[0] = _band_bias(False)
        bias_ref[1] = _band_bias(True)
        causal_ref[...] = _causal_bias()

    qf[...] = q_ref[...].astype(f32)
    kf[...] = k_ref[...].astype(f32)
    vf[...] = v_ref[...].astype(f32)

    def rows2(j):
        r, jb = j // nb2, j % nb2
        cur = r + DIL_2 * WINDOW * jb
        prev = r + DIL_2 * WINDOW * jnp.maximum(jb - 1, 0)
        return pl.ds(prev, WINDOW, stride=DIL_2), pl.ds(cur, WINDOW, stride=DIL_2), jb == 0

    def rows3(j):
        return pl.ds(j, WINDOW, stride=DIL_3)

    def gather_kv(ref, ref_f, j):
        prev1, cur1 = _block_rows(j)
        prev2, cur2, _ = rows2(j)
        return (jnp.concatenate([ref[prev1, :], ref[cur1, :]], axis=0),
                jnp.concatenate([ref_f[prev2, :], ref_f[cur2, :]], axis=0).astype(bf16),
                ref_f[rows3(j), :].astype(bf16))

    def qk(j, slot):
        _, cur1 = _block_rows(j)
        _, cur2, first2 = rows2(j)
        qs = (q_ref[cur1, :], qf[cur2, :].astype(bf16), qf[rows3(j), :].astype(bf16))
        ks = gather_kv(k_ref, kf, j)
        biases = (bias_ref[jnp.where(j == 0, 1, 0)], bias_ref[jnp.where(first2, 1, 0)], causal_ref[...])
        for c in range(DIL_PATTERNS):
            for h, k_h in enumerate(_per_head(ks[c], 0)):
                s = _nt_dot(qs[c], k_h) + biases[c]
                s_buf[slot, 2 * c + h, :, 0:s.shape[1]] = s

    def softmax(j, slot):
        _, cur1 = _block_rows(j)
        _, cur2, _ = rows2(j)
        for c, (m_ref, rows) in enumerate(((m1, cur1), (m2, cur2), (m3, rows3(j)))):
            nk = WINDOW if c == 2 else 2 * WINDOW
            ms = []
            for h in (0, 1):
                s = s_buf[slot, 2 * c + h, :, 0:nk]
                m = jnp.max(s, axis=-1, keepdims=True)
                p_buf[slot, 2 * c + h, :, 0:nk] = jnp.exp2(s - m).astype(bf16)
                ms.append(m)
            m_ref[rows, :] = jnp.where(lo, ms[0], ms[1])

    def pv(j, slot):
        _, cur1 = _block_rows(j)
        _, cur2, _ = rows2(j)
        vs = gather_kv(v_ref, vf, j)
        dests = ((num1, den1, cur1), (num2, den2, cur2), (num3, den3, rows3(j)))
        for c, (num_ref, den_ref, rows) in enumerate(dests):
            nk = WINDOW if c == 2 else 2 * WINDOW
            accs = [jnp.dot(p_buf[slot, 2 * c + h, :, 0:nk], v_h, preferred_element_type=f32)
                    for h, v_h in enumerate(_per_head(vs[c], 1))]
            num, den = _join_pair(*accs)
            num_ref[rows, :] = num
            den_ref[rows, :] = den

    _run_pipeline(seq // WINDOW, qk, softmax, pv, DIL_STEPS)

    ct = 256

    def combine(c, _):
        rows = pl.ds(pl.multiple_of(c * ct, ct), ct)
        ma, mb, mc = m1[rows, :], m2[rows, :], m3[rows, :]
        mx = jnp.maximum(jnp.maximum(ma, mb), mc)
        wa, wb, wc = jnp.exp2(ma - mx), jnp.exp2(mb - mx), jnp.exp2(mc - mx)
        num = wa * num1[rows, :] + wb * num2[rows, :] + wc * num3[rows, :]
        den = wa * den1[rows, :] + wb * den2[rows, :] + wc * den3[rows, :]
        o_ref[rows, :] = (num / den).astype(bf16)
        return 0

    lax.fori_loop(0, seq // ct, combine, 0)


def _dil(proj, batch, seq):
    m = batch * seq
    npair = D_MODEL // LANES
    assert seq == DIL_3 * WINDOW, "pattern 3 is written for one 128-token block per subsequence"
    spec = lambda col: pl.BlockSpec((seq, LANES), lambda b, p: (b, col // LANES + p))
    nh = 2 * DIL_PATTERNS
    return pl.pallas_call(
        _dil_kernel,
        grid=(batch, npair),
        in_specs=[spec(COL_QC), spec(COL_KC), spec(COL_VC)],
        out_specs=pl.BlockSpec((seq, LANES), lambda b, p: (b, p)),
        out_shape=jax.ShapeDtypeStruct((m, D_MODEL), bf16),
        scratch_shapes=[pltpu.VMEM((seq, LANES), f32)] * 12
        + [pltpu.VMEM((2, WINDOW, 2 * WINDOW), f32), pltpu.VMEM((WINDOW, WINDOW), f32),
           pltpu.VMEM((2, nh, WINDOW, 2 * WINDOW), f32), pltpu.VMEM((2, nh, WINDOW, 2 * WINDOW), bf16)],
        compiler_params=_cparams(2),
        name="dil",
    )(proj, proj, proj)


def _layer_norm(z, g, b):
    mu = jnp.mean(z, axis=-1, keepdims=True)
    zc = z - mu
    var = jnp.mean(zc * zc, axis=-1, keepdims=True)
    return zc * lax.rsqrt(var + LN_EPS) * g + b


def _merge_kernel(x_ref, ya_ref, yb_ref, yc_ref, gates_ref, wb_ref, wo_ref, g_ref, b_ref, o_ref):
    for t in range(ROW_TM // ROW_SUB):
        rows = slice(t * ROW_SUB, (t + 1) * ROW_SUB)
        merged = None
        for n, y_ref in enumerate((ya_ref, yb_ref, yc_ref)):
            branch = jnp.dot(y_ref[rows, :], wb_ref[n], preferred_element_type=f32)
            gate = jax.nn.sigmoid(gates_ref[rows, n * D_MODEL:(n + 1) * D_MODEL].astype(f32))
            term = gate * branch
            merged = term if merged is None else merged + term
        mix = jnp.dot(merged.astype(bf16), wo_ref[...], preferred_element_type=f32)
        o_ref[rows, :] = _layer_norm(ALPHA * x_ref[rows, :] + mix, g_ref[...], b_ref[...])


def _merge(x, ya, yb, yc, proj, wb, wo, g, b):
    m = x.shape[0]
    row = lambda width, col_block: pl.BlockSpec((ROW_TM, width), lambda i: (i, col_block))
    gw = N_BRANCHES * D_MODEL
    return pl.pallas_call(
        _merge_kernel,
        grid=(m // ROW_TM,),
        in_specs=[row(D_MODEL, 0), row(D_MODEL, 0), row(D_MODEL, 0), row(D_MODEL, 0), row(gw, COL_GATES // gw),
                  _resident((N_BRANCHES, D_MODEL, D_MODEL), lambda i: (0, 0, 0)),
                  _resident((D_MODEL, D_MODEL), lambda i: (0, 0)),
                  _resident((1, D_MODEL), lambda i: (0, 0)),
                  _resident((1, D_MODEL), lambda i: (0, 0))],
        out_specs=row(D_MODEL, 0),
        out_shape=jax.ShapeDtypeStruct((m, D_MODEL), f32),
        compiler_params=_cparams(1),
        name="merge",
    )(x, ya, yb, yc, proj, wb, wo, g, b)


def _ffn_kernel(x_ref, win_ref, wout_ref, g_ref, b_ref, o_ref):
    for t in range(ROW_TM // ROW_SUB):
        rows = slice(t * ROW_SUB, (t + 1) * ROW_SUB)
        x = x_ref[rows, :]
        xb = x.astype(bf16)
        acc = None
        for c in range(FF_HIDDEN // FFN_CH):
            h1 = jnp.dot(xb, win_ref[:, c * FFN_CH:(c + 1) * FFN_CH], preferred_element_type=f32)
            h3 = jnp.dot(xb, win_ref[:, FF_HIDDEN + c * FFN_CH:FF_HIDDEN + (c + 1) * FFN_CH],
                         preferred_element_type=f32)
            act = (jax.nn.silu(h1) * h3).astype(bf16)
            part = jnp.dot(act, wout_ref[c * FFN_CH:(c + 1) * FFN_CH, :], preferred_element_type=f32)
            acc = part if acc is None else acc + part
        o_ref[rows, :] = _layer_norm(ALPHA * x + acc, g_ref[...], b_ref[...])


def _ffn(x, win, wout, g, b):
    m = x.shape[0]
    return pl.pallas_call(
        _ffn_kernel,
        grid=(m // ROW_TM,),
        in_specs=[pl.BlockSpec((ROW_TM, D_MODEL), lambda i: (i, 0)),
                  _resident((D_MODEL, 2 * FF_HIDDEN), lambda i: (0, 0)),
                  _resident((FF_HIDDEN, D_MODEL), lambda i: (0, 0)),
                  _resident((1, D_MODEL), lambda i: (0, 0)),
                  _resident((1, D_MODEL), lambda i: (0, 0))],
        out_specs=pl.BlockSpec((ROW_TM, D_MODEL), lambda i: (i, 0)),
        out_shape=jax.ShapeDtypeStruct((m, D_MODEL), f32),
        compiler_params=_cparams(1),
        name="ffn",
    )(x, win, wout, g, b)


def _prep_w_in(w_in):
    scale = HEAD_DIM ** -0.5 * LOG2E
    lru_x, lru_g, qb, kb, vb, qc, kc, vc, gates = jnp.split(
        w_in, [1024, 2048, 3072, 3328, 3584, 4608, 5632, 6656], axis=-1)
    w = jnp.concatenate([lru_x, lru_g, gates, qb * scale, qc * scale, kc, vc, kb, vb], axis=-1)
    return w.astype(bf16)


def _block_diag(w):
    per = LRU_CW // HEAD_DIM
    w = w.reshape(DEPTH, LRU_BLOCKS // per, per, HEAD_DIM, HEAD_DIM)
    eye = jnp.eye(per, dtype=w.dtype)
    bd = jnp.einsum("lgaij,ab->lgaibj", w, eye)
    return bd.reshape(DEPTH, LRU_BLOCKS // per, LRU_CW, LRU_CW).astype(bf16)


def kernel(x, w_in, conv_w, conv_b, w_rg, b_rg, w_ig, b_ig, lru_lambda, sinks, w_branch, w_out,
           ln1_g, ln1_b, w_ffn_in, w_ffn_out, ln2_g, ln2_b):
    batch, seq, d = x.shape
    assert d == D_MODEL and w_in.shape == (DEPTH, D_MODEL, IN_WIDTH)
    w_in_b = _prep_w_in(w_in)
    wr_b = _block_diag(w_rg)
    wi_b = _block_diag(w_ig)
    wb_b = w_branch.astype(bf16)
    wo_b = w_out.astype(bf16)
    wfi_b = w_ffn_in.astype(bf16)
    wfo_b = w_ffn_out.astype(bf16)
    row = lambda p, l: p[l].reshape(1, D_MODEL)

    h = x.reshape(batch * seq, D_MODEL)
    for l in range(DEPTH):
        proj, ya = _proj_lru(h, w_in_b[l], conv_w[l], row(conv_b, l), wr_b[l], row(b_rg, l), wi_b[l],
                             row(b_ig, l), row(lru_lambda, l), seq)
        yb = _swa(proj, sinks[l], batch, seq)
        yc = _dil(proj, batch, seq)
        h = _merge(h, ya, yb, yc, proj, wb_b[l], wo_b[l], row(ln1_g, l), row(ln1_b, l))
        h = _ffn(h, wfi_b[l], wfo_b[l], row(ln2_g, l), row(ln2_b, l))
    return h.reshape(batch, seq, D_MODEL)
```

```python
import functools

import jax
import jax.numpy as jnp
from jax import lax
from jax.experimental import pallas as pl
from jax.experimental.pallas import tpu as pltpu

f32 = jnp.float32
bf16 = jnp.bfloat16

D_MODEL = 1024
DEPTH = 4
HEAD_DIM = 64
N_BRANCHES = 3
LRU_BLOCKS = 16
CONV_WIDTH = 4
LRU_C = 8.0
WINDOW = 128
DIL_2, DIL_3 = 4, 16
FF_HIDDEN = 2816
ALPHA = (2.0 * DEPTH) ** 0.25
LN_EPS = 1e-5
NEG_INF = -1e30
LOG2E = 1.4426950408889634

LANES = 128
SUBLANES = 8
VMEM_LIMIT = 56 * 1024 * 1024

IN_WIDTH = 9728
LRU_COLS = 2 * D_MODEL
REST_WIDTH = IN_WIDTH - LRU_COLS
COL_GATES = 0
COL_QB = 3072
COL_QC = 4096
COL_KC = 5120
COL_VC = 6144
COL_KB = 7168
COL_VB = 7424

PROJ_TM = 512
PROJ_TN = 512
LRU_CW = 256
LRU_T = 128
ROW_TM = 1024
ROW_SUB = 512
FFN_CH = 256
SWA_PAIRS = 4
DIL_PATTERNS = 3
SWA_STEPS = 2
DIL_STEPS = 4


def _cparams(n_grid):
    return pltpu.CompilerParams(dimension_semantics=("arbitrary",) * n_grid, vmem_limit_bytes=VMEM_LIMIT)


def _resident(block_shape, index_map):
    return pl.BlockSpec(block_shape, index_map, pipeline_mode=pl.Buffered(1))


def _conv(xe, cw, cb):
    xc = cb + cw[CONV_WIDTH - 1:CONV_WIDTH] * xe[SUBLANES:]
    for s in range(1, CONV_WIDTH):
        xc = xc + cw[CONV_WIDTH - 1 - s:CONV_WIDTH - s] * pltpu.roll(xe, s, 0)[SUBLANES:]
    return xc


def _lru_rows(xc, r_pre, i_pre, gate, carry, br, bi, sp):
    groups = LRU_T // SUBLANES
    row8 = lax.broadcasted_iota(jnp.int32, (groups, SUBLANES, LRU_CW), 1)
    r = jax.nn.sigmoid(r_pre + br)
    i = jax.nn.sigmoid(i_pre + bi)
    log_a = (-LRU_C * r) * sp
    a = jnp.exp(log_a)
    mult = jnp.sqrt(1.0 - a * a)
    b = mult * (i * xc)
    a3 = a.reshape(groups, SUBLANES, LRU_CW)
    b3 = b.reshape(groups, SUBLANES, LRU_CW)
    for d in (1, 2, 4):
        a_s = jnp.where(row8 >= d, pltpu.roll(a3, d, 1), 1.0)
        b_s = jnp.where(row8 >= d, pltpu.roll(b3, d, 1), 0.0)
        b3 = a3 * b_s + b3
        a3 = a3 * a_s
    hs = []
    for g in range(groups):
        h8 = a3[g] * carry + b3[g]
        carry = h8[SUBLANES - 1:SUBLANES, :]
        hs.append(h8)
    return (jnp.concatenate(hs, axis=0) * jax.nn.gelu(gate)).astype(bf16), carry


def _proj_lru_kernel(tiles_per_seq, x_ref, w_ref, cw_ref, cb_ref, wr_ref, br_ref, wi_ref, bi_ref, lam_ref,
                     rest_ref, ya_ref, lx_ref, gt_ref, carry_ref):
    i = pl.program_id(0)
    slot = i % 2
    rslot = 1 - slot
    n_rc = PROJ_TM // LRU_T
    n_cg = D_MODEL // LRU_CW
    n_rest = REST_WIDTH // PROJ_TN
    assert n_rc * n_cg == n_rest + 1

    @pl.when(i == 0)
    def _():
        lx_ref[...] = jnp.zeros(lx_ref.shape, f32)
        gt_ref[...] = jnp.zeros(gt_ref.shape, f32)
        carry_ref[...] = jnp.zeros(carry_ref.shape, f32)

    prev_starts_seq = (i - 1) % tiles_per_seq == 0
    this_starts_seq = i % tiles_per_seq == 0
    xb = x_ref[...].astype(bf16)

    def lru_group(cg):
        lanes = slice(cg * LRU_CW, (cg + 1) * LRU_CW)
        xc = _conv(lx_ref[rslot, :, lanes], cw_ref[:, lanes], cb_ref[:, lanes])
        xcb = xc.astype(bf16)
        r_pre = jnp.dot(xcb, wr_ref[cg], preferred_element_type=f32)
        i_pre = jnp.dot(xcb, wi_ref[cg], preferred_element_type=f32)
        sp = jax.nn.softplus(-lam_ref[:, lanes])
        state = {"carry": jnp.where(prev_starts_seq, 0.0, carry_ref[:, lanes])}

        def finish(rc):
            rows = slice(rc * LRU_T, (rc + 1) * LRU_T)
            y, state["carry"] = _lru_rows(xc[rows], r_pre[rows], i_pre[rows], gt_ref[rslot, rows, lanes],
                                          state["carry"], br_ref[:, lanes], bi_ref[:, lanes], sp)
            ya_ref[rows, lanes] = y
            if rc == n_rc - 1:
                carry_ref[:, lanes] = state["carry"]

        return finish

    finish = lru_group(0)
    lx_ref[slot, SUBLANES:, :] = jnp.dot(xb, w_ref[:, 0:D_MODEL], preferred_element_type=f32)
    gt_ref[slot] = jnp.dot(xb, w_ref[:, D_MODEL:LRU_COLS], preferred_element_type=f32)
    tail = lx_ref[rslot, PROJ_TM:PROJ_TM + SUBLANES, :]
    lx_ref[slot, 0:SUBLANES, :] = jnp.where(this_starts_seq, 0.0, tail)

    for c in range(n_rc * n_cg):
        cg, rc = c // n_rc, c % n_rc
        if rc == 0 and cg > 0:
            finish = lru_group(cg)
        if c < n_rest:
            cols = slice(c * PROJ_TN, (c + 1) * PROJ_TN)
            acc = jnp.dot(xb, w_ref[:, LRU_COLS + c * PROJ_TN:LRU_COLS + (c + 1) * PROJ_TN],
                          preferred_element_type=f32)
            rest_ref[:, cols] = acc.astype(bf16)
        finish(rc)


def _proj_lru(x, w, cw, cb, wr, br, wi, bi, lam, seq):
    m = x.shape[0]
    nt = m // PROJ_TM
    last = nt - 1
    vec = lambda rows: _resident((rows, D_MODEL), lambda i: (0, 0))
    mat = _resident((D_MODEL // LRU_CW, LRU_CW, LRU_CW), lambda i: (0, 0, 0))
    return pl.pallas_call(
        functools.partial(_proj_lru_kernel, seq // PROJ_TM),
        grid=(nt + 1,),
        in_specs=[pl.BlockSpec((PROJ_TM, D_MODEL), lambda i: (jnp.minimum(i, last), 0)),
                  _resident((D_MODEL, IN_WIDTH), lambda i: (0, 0)),
                  vec(CONV_WIDTH), vec(1), mat, vec(1), mat, vec(1), vec(1)],
        out_specs=[pl.BlockSpec((PROJ_TM, REST_WIDTH), lambda i: (jnp.minimum(i, last), 0)),
                   pl.BlockSpec((PROJ_TM, D_MODEL), lambda i: (jnp.maximum(i - 1, 0), 0))],
        out_shape=[jax.ShapeDtypeStruct((m, REST_WIDTH), bf16), jax.ShapeDtypeStruct((m, D_MODEL), bf16)],
        scratch_shapes=[pltpu.VMEM((2, PROJ_TM + SUBLANES, D_MODEL), f32),
                        pltpu.VMEM((2, PROJ_TM, D_MODEL), f32),
                        pltpu.VMEM((1, D_MODEL), f32)],
        compiler_params=_cparams(1),
        name="proj_lru",
    )(x, w, cw, cb, wr, br, wi, bi, lam)


def _band_bias(first_block):
    qi = lax.broadcasted_iota(jnp.int32, (WINDOW, 2 * WINDOW), 0)
    kj = lax.broadcasted_iota(jnp.int32, (WINDOW, 2 * WINDOW), 1)
    rel = qi + WINDOW - kj
    ok = jnp.logical_and(rel >= 0, rel <= WINDOW)
    if first_block:
        ok = jnp.logical_and(ok, kj >= WINDOW)
    return jnp.where(ok, 0.0, NEG_INF).astype(f32)


def _causal_bias():
    qi = lax.broadcasted_iota(jnp.int32, (WINDOW, WINDOW), 0)
    kj = lax.broadcasted_iota(jnp.int32, (WINDOW, WINDOW), 1)
    return jnp.where(qi >= kj, 0.0, NEG_INF).astype(f32)


def _nt_dot(a, b):
    return lax.dot_general(a, b, (((1,), (1,)), ((), ())), preferred_element_type=f32)


def _lane_lo():
    return lax.broadcasted_iota(jnp.int32, (1, LANES), 1) < HEAD_DIM


def _per_head(x, fill):
    lo = _lane_lo()
    f = jnp.full_like(x, fill)
    return jnp.where(lo, x, f), jnp.where(lo, f, x)


def _join_pair(a0, a1):
    lo = _lane_lo()
    return jnp.where(lo, a0, a1), pltpu.roll(jnp.where(lo, a1, a0), HEAD_DIM, 1)


def _run_pipeline(nblk, qk, softmax, pv, steps_per_trip):
    assert steps_per_trip % 2 == 0 and nblk % steps_per_trip == 0 and nblk >= 2 * steps_per_trip

    def step(t, slot):
        pv(t - 2, slot)
        qk(t, slot)
        softmax(t - 1, 1 - slot)

    qk(0, 0)
    qk(1, 1)
    softmax(0, 0)
    for t in range(2, steps_per_trip):
        step(t, t % 2)

    def trip(i, _):
        for u in range(steps_per_trip):
            step(steps_per_trip * i + u, u % 2)
        return 0

    lax.fori_loop(1, nblk // steps_per_trip, trip, 0)
    pv(nblk - 2, 0)
    softmax(nblk - 1, 1)
    pv(nblk - 1, 1)


def _block_rows(j):
    if isinstance(j, int):
        return pl.ds(max(j - 1, 0) * WINDOW, WINDOW), pl.ds(j * WINDOW, WINDOW)
    cur = pl.multiple_of(j * WINDOW, WINDOW)
    prev = pl.multiple_of(jnp.maximum(j - 1, 0) * WINDOW, WINDOW)
    return pl.ds(prev, WINDOW), pl.ds(cur, WINDOW)


def _swa_kernel(sink_ref, q_ref, k_ref, v_ref, o_ref, bias_ref, s_buf, p_buf, e_buf):
    seq = q_ref.shape[0]
    kvp = pl.program_id(1)
    lo = _lane_lo()

    @pl.when(jnp.logical_and(pl.program_id(0) == 0, kvp == 0))
    def _():
        bias_ref[0] = _band_bias(False)
        bias_ref[1] = _band_bias(True)

    def kv_per_query_head(ref, j, fill):
        prev, cur = _block_rows(j)
        xx = jnp.concatenate([ref[prev, :], ref[cur, :]], axis=0)
        xr = pltpu.roll(xx, HEAD_DIM, 1)
        f = jnp.full_like(xx, fill)
        return ((jnp.where(lo, xx, f), jnp.where(lo, f, xr)),
                (jnp.where(lo, xr, f), jnp.where(lo, f, xx)))

    for kvh in (0, 1):
        pairs = (2 * kvh, 2 * kvh + 1)

        def qk(j, slot, kvh=kvh, pairs=pairs):
            _, cur = _block_rows(j)
            bias = bias_ref[jnp.where(j == 0, 1, 0)]
            ks = kv_per_query_head(k_ref, j, 0)[kvh]
            for pair in pairs:
                q2 = q_ref[cur, pair * LANES:(pair + 1) * LANES]
                for h in (0, 1):
                    s_buf[slot, 2 * pair + h] = _nt_dot(q2, ks[h]) + bias

        def softmax(j, slot, pairs=pairs):
            for pair in pairs:
                sink_terms = []
                for h in (0, 1):
                    sink = sink_ref[kvp * 2 * SWA_PAIRS + 2 * pair + h] * LOG2E
                    s = s_buf[slot, 2 * pair + h]
                    m = jnp.maximum(jnp.max(s, axis=-1, keepdims=True), sink)
                    p_buf[slot, 2 * pair + h] = jnp.exp2(s - m).astype(bf16)
                    sink_terms.append(jnp.exp2(sink - m))
                e_buf[slot, pair] = jnp.where(lo, sink_terms[0], sink_terms[1])

        def pv(j, slot, kvh=kvh, pairs=pairs):
            _, cur = _block_rows(j)
            vs = kv_per_query_head(v_ref, j, 1)[kvh]
            for pair in pairs:
                accs = [jnp.dot(p_buf[slot, 2 * pair + h], vs[h], preferred_element_type=f32) for h in (0, 1)]
                num, den = _join_pair(*accs)
                den = den + e_buf[slot, pair]
                o_ref[cur, pair * LANES:(pair + 1) * LANES] = (num / den).astype(bf16)

        _run_pipeline(seq // WINDOW, qk, softmax, pv, SWA_STEPS)


def _swa(proj, sinks, batch, seq):
    m = batch * seq
    qw = SWA_PAIRS * LANES
    nh = 2 * SWA_PAIRS
    return pl.pallas_call(
        _swa_kernel,
        grid=(batch, 2),
        in_specs=[pl.BlockSpec(memory_space=pltpu.SMEM),
                  pl.BlockSpec((seq, qw), lambda b, p: (b, COL_QB // qw + p)),
                  pl.BlockSpec((seq, LANES), lambda b, p: (b, COL_KB // LANES + p)),
                  pl.BlockSpec((seq, LANES), lambda b, p: (b, COL_VB // LANES + p))],
        out_specs=pl.BlockSpec((seq, qw), lambda b, p: (b, p)),
        out_shape=jax.ShapeDtypeStruct((m, D_MODEL), bf16),
        scratch_shapes=[pltpu.VMEM((2, WINDOW, 2 * WINDOW), f32),
                        pltpu.VMEM((2, nh, WINDOW, 2 * WINDOW), f32),
                        pltpu.VMEM((2, nh, WINDOW, 2 * WINDOW), bf16),
                        pltpu.VMEM((2, SWA_PAIRS, WINDOW, LANES), f32)],
        compiler_params=_cparams(2),
        name="swa",
    )(sinks, proj, proj, proj)


def _dil_kernel(q_ref, k_ref, v_ref, o_ref, qf, kf, vf, q4, k4, v4, q4b, k4b, v4b,
                num1, den1, m1, num2, den2, m2, num3, den3, m3, out_f, bias_ref, causal_ref, s_buf, p_buf):
    seq = q_ref.shape[0]
    lo = _lane_lo()
    sub = seq // DIL_2
    nb2 = sub // WINDOW

    @pl.when(jnp.logical_and(pl.program_id(0) == 0, pl.program_id(1) == 0))
    def _():
        bias_ref[0] = _band_bias(False)
        bias_ref[1] = _band_bias(True)
        causal_ref[...] = _causal_bias()

    for src, nat, four, four_b in ((q_ref, qf, q4, q4b), (k_ref, kf, k4, k4b), (v_ref, vf, v4, v4b)):
        nat[...] = src[...].astype(f32)
        for r in range(DIL_2):
            part = nat[pl.ds(r, sub, stride=DIL_2), :]
            four[r * sub:(r + 1) * sub, :] = part
            four_b[r * sub:(r + 1) * sub, :] = part.astype(bf16)

    def rows2(j):
        r, jb = j // nb2, j % nb2
        cur = pl.multiple_of(r * sub + jb * WINDOW, WINDOW)
        prev = pl.multiple_of(r * sub + jnp.maximum(jb - 1, 0) * WINDOW, WINDOW)
        return pl.ds(prev, WINDOW), pl.ds(cur, WINDOW), jb == 0

    def rows3(j):
        return pl.ds((j % DIL_2) * sub + j // DIL_2, WINDOW, stride=DIL_2)

    def gather_kv(ref, ref4, ref4b, j):
        prev1, cur1 = _block_rows(j)
        prev2, cur2, _ = rows2(j)
        return (jnp.concatenate([ref[prev1, :], ref[cur1, :]], axis=0),
                jnp.concatenate([ref4b[prev2, :], ref4b[cur2, :]], axis=0),
                ref4[rows3(j), :].astype(bf16))

    def qk(j, slot):
        _, cur1 = _block_rows(j)
        _, cur2, first2 = rows2(j)
        qs = (q_ref[cur1, :], q4b[cur2, :], q4[rows3(j), :].astype(bf16))
        ks = gather_kv(k_ref, k4, k4b, j)
        biases = (bias_ref[jnp.where(j == 0, 1, 0)], bias_ref[jnp.where(first2, 1, 0)], causal_ref[...])
        for c in range(DIL_PATTERNS):
            for h, k_h in enumerate(_per_head(ks[c], 0)):
                s = _nt_dot(qs[c], k_h) + biases[c]
                s_buf[slot, 2 * c + h, :, 0:s.shape[1]] = s

    def softmax(j, slot):
        _, cur1 = _block_rows(j)
        _, cur2, _ = rows2(j)
        for c, (m_ref, rows) in enumerate(((m1, cur1), (m2, cur2), (m3, rows3(j)))):
            nk = WINDOW if c == 2 else 2 * WINDOW
            ms = []
            for h in (0, 1):
                s = s_buf[slot, 2 * c + h, :, 0:nk]
                m = jnp.max(s, axis=-1, keepdims=True)
                p_buf[slot, 2 * c + h, :, 0:nk] = jnp.exp2(s - m).astype(bf16)
                ms.append(m)
            m_ref[rows, :] = jnp.where(lo, ms[0], ms[1])

    def pv(j, slot):
        _, cur1 = _block_rows(j)
        _, cur2, _ = rows2(j)
        vs = gather_kv(v_ref, v4, v4b, j)
        dests = ((num1, den1, cur1), (num2, den2, cur2), (num3, den3, rows3(j)))
        for c, (num_ref, den_ref, rows) in enumerate(dests):
            nk = WINDOW if c == 2 else 2 * WINDOW
            accs = [jnp.dot(p_buf[slot, 2 * c + h, :, 0:nk], v_h, preferred_element_type=f32)
                    for h, v_h in enumerate(_per_head(vs[c], 1))]
            num, den = _join_pair(*accs)
            num_ref[rows, :] = num
            den_ref[rows, :] = den

    _run_pipeline(seq // WINDOW, qk, softmax, pv, DIL_STEPS)

    ct = 256
    per_res = sub // ct

    def combine(c, _):
        r, part = c // per_res, c % per_res
        rows4 = pl.ds(pl.multiple_of(c * ct, ct), ct)
        rows_nat = pl.ds(r + DIL_2 * ct * part, ct, stride=DIL_2)
        ma, mb, mc = m1[rows_nat, :], m2[rows4, :], m3[rows4, :]
        mx = jnp.maximum(jnp.maximum(ma, mb), mc)
        wa, wb, wc = jnp.exp2(ma - mx), jnp.exp2(mb - mx), jnp.exp2(mc - mx)
        num = wa * num1[rows_nat, :] + wb * num2[rows4, :] + wc * num3[rows4, :]
        den = wa * den1[rows_nat, :] + wb * den2[rows4, :] + wc * den3[rows4, :]
        out_f[rows_nat, :] = num / den
        return 0

    lax.fori_loop(0, seq // ct, combine, 0)
    o_ref[...] = out_f[...].astype(bf16)


def _dil(proj, batch, seq):
    m = batch * seq
    npair = D_MODEL // LANES
    assert seq == DIL_3 * WINDOW, "pattern 3 is written for one 128-token block per subsequence"
    spec = lambda col: pl.BlockSpec((seq, LANES), lambda b, p: (b, col // LANES + p))
    nh = 2 * DIL_PATTERNS
    tok_f = pltpu.VMEM((seq, LANES), f32)
    tok_b = pltpu.VMEM((seq, LANES), bf16)
    return pl.pallas_call(
        _dil_kernel,
        grid=(batch, npair),
        in_specs=[spec(COL_QC), spec(COL_KC), spec(COL_VC)],
        out_specs=pl.BlockSpec((seq, LANES), lambda b, p: (b, p)),
        out_shape=jax.ShapeDtypeStruct((m, D_MODEL), bf16),
        scratch_shapes=[tok_f] * 6 + [tok_b] * 3 + [tok_f] * 10
        + [pltpu.VMEM((2, WINDOW, 2 * WINDOW), f32), pltpu.VMEM((WINDOW, WINDOW), f32),
           pltpu.VMEM((2, nh, WINDOW, 2 * WINDOW), f32), pltpu.VMEM((2, nh, WINDOW, 2 * WINDOW), bf16)],
        compiler_params=_cparams(2),
        name="dil",
    )(proj, proj, proj)


def _layer_norm(z, g, b):
    mu = jnp.mean(z, axis=-1, keepdims=True)
    zc = z - mu
    var = jnp.mean(zc * zc, axis=-1, keepdims=True)
    return zc * lax.rsqrt(var + LN_EPS) * g + b


def _merge_kernel(x_ref, ya_ref, yb_ref, yc_ref, gates_ref, wb_ref, wo_ref, g_ref, b_ref, o_ref):
    for t in range(ROW_TM // ROW_SUB):
        rows = slice(t * ROW_SUB, (t + 1) * ROW_SUB)
        merged = None
        for n, y_ref in enumerate((ya_ref, yb_ref, yc_ref)):
            branch = jnp.dot(y_ref[rows, :], wb_ref[n], preferred_element_type=f32)
            gate = jax.nn.sigmoid(gates_ref[rows, n * D_MODEL:(n + 1) * D_MODEL].astype(f32))
            term = gate * branch
            merged = term if merged is None else merged + term
        mix = jnp.dot(merged.astype(bf16), wo_ref[...], preferred_element_type=f32)
        o_ref[rows, :] = _layer_norm(ALPHA * x_ref[rows, :] + mix, g_ref[...], b_ref[...])


def _merge(x, ya, yb, yc, proj, wb, wo, g, b):
    m = x.shape[0]
    row = lambda width, col_block: pl.BlockSpec((ROW_TM, width), lambda i: (i, col_block))
    gw = N_BRANCHES * D_MODEL
    return pl.pallas_call(
        _merge_kernel,
        grid=(m // ROW_TM,),
        in_specs=[row(D_MODEL, 0), row(D_MODEL, 0), row(D_MODEL, 0), row(D_MODEL, 0), row(gw, COL_GATES // gw),
                  _resident((N_BRANCHES, D_MODEL, D_MODEL), lambda i: (0, 0, 0)),
                  _resident((D_MODEL, D_MODEL), lambda i: (0, 0)),
                  _resident((1, D_MODEL), lambda i: (0, 0)),
                  _resident((1, D_MODEL), lambda i: (0, 0))],
        out_specs=row(D_MODEL, 0),
        out_shape=jax.ShapeDtypeStruct((m, D_MODEL), f32),
        compiler_params=_cparams(1),
        name="merge",
    )(x, ya, yb, yc, proj, wb, wo, g, b)


def _ffn_kernel(x_ref, win_ref, wout_ref, g_ref, b_ref, o_ref):
    for t in range(ROW_TM // ROW_SUB):
        rows = slice(t * ROW_SUB, (t + 1) * ROW_SUB)
        x = x_ref[rows, :]
        xb = x.astype(bf16)
        acc = None
        for c in range(FF_HIDDEN // FFN_CH):
            h1 = jnp.dot(xb, win_ref[:, c * FFN_CH:(c + 1) * FFN_CH], preferred_element_type=f32)
            h3 = jnp.dot(xb, win_ref[:, FF_HIDDEN + c * FFN_CH:FF_HIDDEN + (c + 1) * FFN_CH],
                         preferred_element_type=f32)
            act = (jax.nn.silu(h1) * h3).astype(bf16)
            part = jnp.dot(act, wout_ref[c * FFN_CH:(c + 1) * FFN_CH, :], preferred_element_type=f32)
            acc = part if acc is None else acc + part
        o_ref[rows, :] = _layer_norm(ALPHA * x + acc, g_ref[...], b_ref[...])


def _ffn(x, win, wout, g, b):
    m = x.shape[0]
    return pl.pallas_call(
        _ffn_kernel,
        grid=(m // ROW_TM,),
        in_specs=[pl.BlockSpec((ROW_TM, D_MODEL), lambda i: (i, 0)),
                  _resident((D_MODEL, 2 * FF_HIDDEN), lambda i: (0, 0)),
                  _resident((FF_HIDDEN, D_MODEL), lambda i: (0, 0)),
                  _resident((1, D_MODEL), lambda i: (0, 0)),
                  _resident((1, D_MODEL), lambda i: (0, 0))],
        out_specs=pl.BlockSpec((ROW_TM, D_MODEL), lambda i: (i, 0)),
        out_shape=jax.ShapeDtypeStruct((m, D_MODEL), f32),
        compiler_params=_cparams(1),
        name="ffn",
    )(x, win, wout, g, b)


def _prep_w_in(w_in):
    scale = HEAD_DIM ** -0.5 * LOG2E
    lru_x, lru_g, qb, kb, vb, qc, kc, vc, gates = jnp.split(
        w_in, [1024, 2048, 3072, 3328, 3584, 4608, 5632, 6656], axis=-1)
    w = jnp.concatenate([lru_x, lru_g, gates, qb * scale, qc * scale, kc, vc, kb, vb], axis=-1)
    return w.astype(bf16)


def _block_diag(w):
    per = LRU_CW // HEAD_DIM
    w = w.reshape(DEPTH, LRU_BLOCKS // per, per, HEAD_DIM, HEAD_DIM)
    eye = jnp.eye(per, dtype=w.dtype)
    bd = jnp.einsum("lgaij,ab->lgaibj", w, eye)
    return bd.reshape(DEPTH, LRU_BLOCKS // per, LRU_CW, LRU_CW).astype(bf16)


def kernel(x, w_in, conv_w, conv_b, w_rg, b_rg, w_ig, b_ig, lru_lambda, sinks, w_branch, w_out,
           ln1_g, ln1_b, w_ffn_in, w_ffn_out, ln2_g, ln2_b):
    batch, seq, d = x.shape
    assert d == D_MODEL and w_in.shape == (DEPTH, D_MODEL, IN_WIDTH)
    w_in_b = _prep_w_in(w_in)
    wr_b = _block_diag(w_rg)
    wi_b = _block_diag(w_ig)
    wb_b = w_branch.astype(bf16)
    wo_b = w_out.astype(bf16)
    wfi_b = w_ffn_in.astype(bf16)
    wfo_b = w_ffn_out.astype(bf16)
    row = lambda p, l: p[l].reshape(1, D_MODEL)

    h = x.reshape(batch * seq, D_MODEL)
    for l in range(DEPTH):
        proj, ya = _proj_lru(h, w_in_b[l], conv_w[l], row(conv_b, l), wr_b[l], row(b_rg, l), wi_b[l],
                             row(b_ig, l), row(lru_lambda, l), seq)
        yb = _swa(proj, sinks[l], batch, seq)
        yc = _dil(proj, batch, seq)
        h = _merge(h, ya, yb, yc, proj, wb_b[l], wo_b[l], row(ln1_g, l), row(ln1_b, l))
        h = _ffn(h, wfi_b[l], wfo_b[l], row(ln2_g, l), row(ln2_b, l))
    return h.reshape(batch, seq, D_MODEL)
```

```python
import functools

import jax
import jax.numpy as jnp
from jax import lax
from jax.experimental import pallas as pl
from jax.experimental.pallas import tpu as pltpu

f32 = jnp.float32
bf16 = jnp.bfloat16

D_MODEL = 1024
DEPTH = 4
HEAD_DIM = 64
N_BRANCHES = 3
LRU_BLOCKS = 16
CONV_WIDTH = 4
LRU_C = 8.0
WINDOW = 128
DIL_2, DIL_3 = 4, 16
FF_HIDDEN = 2816
ALPHA = (2.0 * DEPTH) ** 0.25
LN_EPS = 1e-5
NEG_INF = -1e30
LOG2E = 1.4426950408889634

LANES = 128
SUBLANES = 8
VMEM_LIMIT = 56 * 1024 * 1024

IN_WIDTH = 9728
LRU_COLS = 2 * D_MODEL
REST_WIDTH = IN_WIDTH - LRU_COLS
COL_GATES = 0
COL_QB = 3072
COL_QC = 4096
COL_KC = 5120
COL_VC = 6144
COL_KB = 7168
COL_VB = 7424

PROJ_TM = 512
PROJ_TN = 512
LRU_CW = 256
LRU_T = 128
ROW_TM = 1024
ROW_SUB = 512
FFN_CH = 256
SWA_PAIRS = 4
DIL_PATTERNS = 3
SWA_STEPS = 4
DIL_STEPS = 4


def _cparams(n_grid):
    return pltpu.CompilerParams(dimension_semantics=("arbitrary",) * n_grid, vmem_limit_bytes=VMEM_LIMIT)


def _resident(block_shape, index_map):
    return pl.BlockSpec(block_shape, index_map, pipeline_mode=pl.Buffered(1))


def _conv(xe, cw, cb):
    xc = cb + cw[CONV_WIDTH - 1:CONV_WIDTH] * xe[SUBLANES:]
    for s in range(1, CONV_WIDTH):
        xc = xc + cw[CONV_WIDTH - 1 - s:CONV_WIDTH - s] * pltpu.roll(xe, s, 0)[SUBLANES:]
    return xc


def _lru_rows(xc, r_pre, i_pre, gate, carry, br, bi, sp):
    groups = LRU_T // SUBLANES
    row8 = lax.broadcasted_iota(jnp.int32, (groups, SUBLANES, LRU_CW), 1)
    r = jax.nn.sigmoid(r_pre + br)
    i = jax.nn.sigmoid(i_pre + bi)
    log_a = (-LRU_C * r) * sp
    a = jnp.exp(log_a)
    mult = jnp.sqrt(1.0 - a * a)
    b = mult * (i * xc)
    a3 = a.reshape(groups, SUBLANES, LRU_CW)
    b3 = b.reshape(groups, SUBLANES, LRU_CW)
    for d in (1, 2, 4):
        a_s = jnp.where(row8 >= d, pltpu.roll(a3, d, 1), 1.0)
        b_s = jnp.where(row8 >= d, pltpu.roll(b3, d, 1), 0.0)
        b3 = a3 * b_s + b3
        a3 = a3 * a_s
    hs = []
    for g in range(groups):
        h8 = a3[g] * carry + b3[g]
        carry = h8[SUBLANES - 1:SUBLANES, :]
        hs.append(h8)
    return (jnp.concatenate(hs, axis=0) * jax.nn.gelu(gate)).astype(bf16), carry


def _proj_lru_kernel(tiles_per_seq, x_ref, w_ref, cw_ref, cb_ref, wr_ref, br_ref, wi_ref, bi_ref, lam_ref,
                     rest_ref, ya_ref, lx_ref, gt_ref, carry_ref):
    i = pl.program_id(0)
    slot = i % 2
    rslot = 1 - slot
    n_rc = PROJ_TM // LRU_T
    n_cg = D_MODEL // LRU_CW
    n_rest = REST_WIDTH // PROJ_TN
    assert n_rc * n_cg == n_rest + 1

    @pl.when(i == 0)
    def _():
        lx_ref[...] = jnp.zeros(lx_ref.shape, f32)
        gt_ref[...] = jnp.zeros(gt_ref.shape, f32)
        carry_ref[...] = jnp.zeros(carry_ref.shape, f32)

    prev_starts_seq = (i - 1) % tiles_per_seq == 0
    this_starts_seq = i % tiles_per_seq == 0
    xb = x_ref[...].astype(bf16)

    def lru_group(cg):
        lanes = slice(cg * LRU_CW, (cg + 1) * LRU_CW)
        xc = _conv(lx_ref[rslot, :, lanes], cw_ref[:, lanes], cb_ref[:, lanes])
        xcb = xc.astype(bf16)
        r_pre = jnp.dot(xcb, wr_ref[cg], preferred_element_type=f32)
        i_pre = jnp.dot(xcb, wi_ref[cg], preferred_element_type=f32)
        sp = jax.nn.softplus(-lam_ref[:, lanes])
        state = {"carry": jnp.where(prev_starts_seq, 0.0, carry_ref[:, lanes])}

        def finish(rc):
            rows = slice(rc * LRU_T, (rc + 1) * LRU_T)
            y, state["carry"] = _lru_rows(xc[rows], r_pre[rows], i_pre[rows], gt_ref[rslot, rows, lanes],
                                          state["carry"], br_ref[:, lanes], bi_ref[:, lanes], sp)
            ya_ref[rows, lanes] = y
            if rc == n_rc - 1:
                carry_ref[:, lanes] = state["carry"]

        return finish

    finish = lru_group(0)
    lx_ref[slot, SUBLANES:, :] = jnp.dot(xb, w_ref[:, 0:D_MODEL], preferred_element_type=f32)
    gt_ref[slot] = jnp.dot(xb, w_ref[:, D_MODEL:LRU_COLS], preferred_element_type=f32)
    tail = lx_ref[rslot, PROJ_TM:PROJ_TM + SUBLANES, :]
    lx_ref[slot, 0:SUBLANES, :] = jnp.where(this_starts_seq, 0.0, tail)

    for c in range(n_rc * n_cg):
        cg, rc = c // n_rc, c % n_rc
        if rc == 0 and cg > 0:
            finish = lru_group(cg)
        if c < n_rest:
            cols = slice(c * PROJ_TN, (c + 1) * PROJ_TN)
            acc = jnp.dot(xb, w_ref[:, LRU_COLS + c * PROJ_TN:LRU_COLS + (c + 1) * PROJ_TN],
                          preferred_element_type=f32)
            rest_ref[:, cols] = acc.astype(bf16)
        finish(rc)


def _proj_lru(x, w, cw, cb, wr, br, wi, bi, lam, seq):
    m = x.shape[0]
    nt = m // PROJ_TM
    last = nt - 1
    vec = lambda rows: _resident((rows, D_MODEL), lambda i: (0, 0))
    mat = _resident((D_MODEL // LRU_CW, LRU_CW, LRU_CW), lambda i: (0, 0, 0))
    return pl.pallas_call(
        functools.partial(_proj_lru_kernel, seq // PROJ_TM),
        grid=(nt + 1,),
        in_specs=[pl.BlockSpec((PROJ_TM, D_MODEL), lambda i: (jnp.minimum(i, last), 0)),
                  _resident((D_MODEL, IN_WIDTH), lambda i: (0, 0)),
                  vec(CONV_WIDTH), vec(1), mat, vec(1), mat, vec(1), vec(1)],
        out_specs=[pl.BlockSpec((PROJ_TM, REST_WIDTH), lambda i: (jnp.minimum(i, last), 0)),
                   pl.BlockSpec((PROJ_TM, D_MODEL), lambda i: (jnp.maximum(i - 1, 0), 0))],
        out_shape=[jax.ShapeDtypeStruct((m, REST_WIDTH), bf16), jax.ShapeDtypeStruct((m, D_MODEL), bf16)],
        scratch_shapes=[pltpu.VMEM((2, PROJ_TM + SUBLANES, D_MODEL), f32),
                        pltpu.VMEM((2, PROJ_TM, D_MODEL), f32),
                        pltpu.VMEM((1, D_MODEL), f32)],
        compiler_params=_cparams(1),
        name="proj_lru",
    )(x, w, cw, cb, wr, br, wi, bi, lam)


def _band_bias(first_block):
    qi = lax.broadcasted_iota(jnp.int32, (WINDOW, 2 * WINDOW), 0)
    kj = lax.broadcasted_iota(jnp.int32, (WINDOW, 2 * WINDOW), 1)
    rel = qi + WINDOW - kj
    ok = jnp.logical_and(rel >= 0, rel <= WINDOW)
    if first_block:
        ok = jnp.logical_and(ok, kj >= WINDOW)
    return jnp.where(ok, 0.0, NEG_INF).astype(f32)


def _causal_bias():
    qi = lax.broadcasted_iota(jnp.int32, (WINDOW, WINDOW), 0)
    kj = lax.broadcasted_iota(jnp.int32, (WINDOW, WINDOW), 1)
    return jnp.where(qi >= kj, 0.0, NEG_INF).astype(f32)


def _nt_dot(a, b):
    return lax.dot_general(a, b, (((1,), (1,)), ((), ())), preferred_element_type=f32)


def _lane_lo():
    return lax.broadcasted_iota(jnp.int32, (1, LANES), 1) < HEAD_DIM


def _per_head(x, fill):
    lo = _lane_lo()
    f = jnp.full_like(x, fill)
    return jnp.where(lo, x, f), jnp.where(lo, f, x)


def _join_pair(a0, a1):
    lo = _lane_lo()
    return jnp.where(lo, a0, a1), pltpu.roll(jnp.where(lo, a1, a0), HEAD_DIM, 1)


def _run_pipeline(nblk, qk, softmax, pv, steps_per_trip):
    assert steps_per_trip % 2 == 0 and nblk % steps_per_trip == 0 and nblk >= 2 * steps_per_trip

    def step(t, slot):
        pv(t - 2, slot)
        qk(t, slot)
        softmax(t - 1, 1 - slot)

    qk(0, 0)
    qk(1, 1)
    softmax(0, 0)
    for t in range(2, steps_per_trip):
        step(t, t % 2)

    def trip(i, _):
        for u in range(steps_per_trip):
            step(steps_per_trip * i + u, u % 2)
        return 0

    lax.fori_loop(1, nblk // steps_per_trip, trip, 0)
    pv(nblk - 2, 0)
    softmax(nblk - 1, 1)
    pv(nblk - 1, 1)


def _block_rows(j):
    if isinstance(j, int):
        return pl.ds(max(j - 1, 0) * WINDOW, WINDOW), pl.ds(j * WINDOW, WINDOW)
    cur = pl.multiple_of(j * WINDOW, WINDOW)
    prev = pl.multiple_of(jnp.maximum(j - 1, 0) * WINDOW, WINDOW)
    return pl.ds(prev, WINDOW), pl.ds(cur, WINDOW)


def _swa_kernel(sink_ref, q_ref, k_ref, v_ref, o_ref, bias_ref, s_buf, p_buf, e_buf, kd_ref, vd_ref):
    seq = q_ref.shape[0]
    kvp = pl.program_id(1)
    lo = _lane_lo()

    @pl.when(jnp.logical_and(pl.program_id(0) == 0, kvp == 0))
    def _():
        bias_ref[0] = _band_bias(False)
        bias_ref[1] = _band_bias(True)

    lane_rolled_k = pltpu.roll(k_ref[...], HEAD_DIM, 1)
    lane_rolled_v = pltpu.roll(v_ref[...], HEAD_DIM, 1)
    for kvh in (0, 1):
        sel = lo if kvh == 0 else jnp.logical_not(lo)
        kd_ref[kvh] = jnp.where(sel, k_ref[...], lane_rolled_k)
        vd_ref[kvh] = jnp.where(sel, v_ref[...], lane_rolled_v)

    for kvh in (0, 1):
        pairs = (2 * kvh, 2 * kvh + 1)

        def kv_blocks(ref, j, fill, kvh=kvh):
            prev, cur = _block_rows(j)
            return _per_head(jnp.concatenate([ref[kvh, prev, :], ref[kvh, cur, :]], axis=0), fill)

        def qk(j, slot, pairs=pairs, kv_blocks=kv_blocks):
            _, cur = _block_rows(j)
            bias = bias_ref[jnp.where(j == 0, 1, 0)]
            ks = kv_blocks(kd_ref, j, 0)
            for pair in pairs:
                q2 = q_ref[cur, pair * LANES:(pair + 1) * LANES]
                for h in (0, 1):
                    s_buf[slot, 2 * pair + h] = _nt_dot(q2, ks[h]) + bias

        def softmax(j, slot, pairs=pairs):
            for pair in pairs:
                sink_terms = []
                for h in (0, 1):
                    sink = sink_ref[kvp * 2 * SWA_PAIRS + 2 * pair + h] * LOG2E
                    s = s_buf[slot, 2 * pair + h]
                    m = jnp.maximum(jnp.max(s, axis=-1, keepdims=True), sink)
                    p_buf[slot, 2 * pair + h] = jnp.exp2(s - m).astype(bf16)
                    sink_terms.append(jnp.exp2(sink - m))
                e_buf[slot, pair] = jnp.where(lo, sink_terms[0], sink_terms[1])

        def pv(j, slot, pairs=pairs, kv_blocks=kv_blocks):
            _, cur = _block_rows(j)
            vs = kv_blocks(vd_ref, j, 1)
            for pair in pairs:
                accs = [jnp.dot(p_buf[slot, 2 * pair + h], vs[h], preferred_element_type=f32) for h in (0, 1)]
                num, den = _join_pair(*accs)
                den = den + e_buf[slot, pair]
                o_ref[cur, pair * LANES:(pair + 1) * LANES] = (num / den).astype(bf16)

        _run_pipeline(seq // WINDOW, qk, softmax, pv, SWA_STEPS)


def _swa(proj, sinks, batch, seq):
    m = batch * seq
    qw = SWA_PAIRS * LANES
    nh = 2 * SWA_PAIRS
    return pl.pallas_call(
        _swa_kernel,
        grid=(batch, 2),
        in_specs=[pl.BlockSpec(memory_space=pltpu.SMEM),
                  pl.BlockSpec((seq, qw), lambda b, p: (b, COL_QB // qw + p)),
                  pl.BlockSpec((seq, LANES), lambda b, p: (b, COL_KB // LANES + p)),
                  pl.BlockSpec((seq, LANES), lambda b, p: (b, COL_VB // LANES + p))],
        out_specs=pl.BlockSpec((seq, qw), lambda b, p: (b, p)),
        out_shape=jax.ShapeDtypeStruct((m, D_MODEL), bf16),
        scratch_shapes=[pltpu.VMEM((2, WINDOW, 2 * WINDOW), f32),
                        pltpu.VMEM((2, nh, WINDOW, 2 * WINDOW), f32),
                        pltpu.VMEM((2, nh, WINDOW, 2 * WINDOW), bf16),
                        pltpu.VMEM((2, SWA_PAIRS, WINDOW, LANES), f32),
                        pltpu.VMEM((2, seq, LANES), bf16), pltpu.VMEM((2, seq, LANES), bf16)],
        compiler_params=_cparams(2),
        name="swa",
    )(sinks, proj, proj, proj)


def _dil_kernel(q_ref, k_ref, v_ref, o_ref, qf, kf, vf, q4, k4, v4, q4b, k4b, v4b,
                num1, den1, m1, num2, den2, m2, num3, den3, m3, out_f, bias_ref, causal_ref, s_buf, p_buf):
    seq = q_ref.shape[0]
    lo = _lane_lo()
    sub = seq // DIL_2
    nb2 = sub // WINDOW

    @pl.when(jnp.logical_and(pl.program_id(0) == 0, pl.program_id(1) == 0))
    def _():
        bias_ref[0] = _band_bias(False)
        bias_ref[1] = _band_bias(True)
        causal_ref[...] = _causal_bias()

    for src, nat, four, four_b in ((q_ref, qf, q4, q4b), (k_ref, kf, k4, k4b), (v_ref, vf, v4, v4b)):
        nat[...] = src[...].astype(f32)
        for r in range(DIL_2):
            part = nat[pl.ds(r, sub, stride=DIL_2), :]
            four[r * sub:(r + 1) * sub, :] = part
            four_b[r * sub:(r + 1) * sub, :] = part.astype(bf16)

    def rows2(j):
        r, jb = j // nb2, j % nb2
        cur = pl.multiple_of(r * sub + jb * WINDOW, WINDOW)
        prev = pl.multiple_of(r * sub + jnp.maximum(jb - 1, 0) * WINDOW, WINDOW)
        return pl.ds(prev, WINDOW), pl.ds(cur, WINDOW), jb == 0

    def rows3(j):
        return pl.ds((j % DIL_2) * sub + j // DIL_2, WINDOW, stride=DIL_2)

    def gather_kv(ref, ref4, ref4b, j):
        prev1, cur1 = _block_rows(j)
        prev2, cur2, _ = rows2(j)
        return (jnp.concatenate([ref[prev1, :], ref[cur1, :]], axis=0),
                jnp.concatenate([ref4b[prev2, :], ref4b[cur2, :]], axis=0),
                ref4[rows3(j), :].astype(bf16))

    def qk(j, slot):
        _, cur1 = _block_rows(j)
        _, cur2, first2 = rows2(j)
        qs = (q_ref[cur1, :], q4b[cur2, :], q4[rows3(j), :].astype(bf16))
        ks = gather_kv(k_ref, k4, k4b, j)
        biases = (bias_ref[jnp.where(j == 0, 1, 0)], bias_ref[jnp.where(first2, 1, 0)], causal_ref[...])
        for c in range(DIL_PATTERNS):
            for h, k_h in enumerate(_per_head(ks[c], 0)):
                s = _nt_dot(qs[c], k_h) + biases[c]
                s_buf[slot, 2 * c + h, :, 0:s.shape[1]] = s

    def softmax(j, slot):
        _, cur1 = _block_rows(j)
        _, cur2, _ = rows2(j)
        for c, (m_ref, rows) in enumerate(((m1, cur1), (m2, cur2), (m3, rows3(j)))):
            nk = WINDOW if c == 2 else 2 * WINDOW
            ms = []
            for h in (0, 1):
                s = s_buf[slot, 2 * c + h, :, 0:nk]
                m = jnp.max(s, axis=-1, keepdims=True)
                p_buf[slot, 2 * c + h, :, 0:nk] = jnp.exp2(s - m).astype(bf16)
                ms.append(m)
            m_ref[rows, :] = jnp.where(lo, ms[0], ms[1])

    def pv(j, slot):
        _, cur1 = _block_rows(j)
        _, cur2, _ = rows2(j)
        vs = gather_kv(v_ref, v4, v4b, j)
        dests = ((num1, den1, cur1), (num2, den2, cur2), (num3, den3, rows3(j)))
        for c, (num_ref, den_ref, rows) in enumerate(dests):
            nk = WINDOW if c == 2 else 2 * WINDOW
            accs = [jnp.dot(p_buf[slot, 2 * c + h, :, 0:nk], v_h, preferred_element_type=f32)
                    for h, v_h in enumerate(_per_head(vs[c], 1))]
            num, den = _join_pair(*accs)
            num_ref[rows, :] = num
            den_ref[rows, :] = den

    _run_pipeline(seq // WINDOW, qk, softmax, pv, DIL_STEPS)

    ct = 256
    per_res = sub // ct

    def combine(c, _):
        r, part = c // per_res, c % per_res
        rows4 = pl.ds(pl.multiple_of(c * ct, ct), ct)
        rows_nat = pl.ds(r + DIL_2 * ct * part, ct, stride=DIL_2)
        ma, mb, mc = m1[rows_nat, :], m2[rows4, :], m3[rows4, :]
        mx = jnp.maximum(jnp.maximum(ma, mb), mc)
        wa, wb, wc = jnp.exp2(ma - mx), jnp.exp2(mb - mx), jnp.exp2(mc - mx)
        num = wa * num1[rows_nat, :] + wb * num2[rows4, :] + wc * num3[rows4, :]
        den = wa * den1[rows_nat, :] + wb * den2[rows4, :] + wc * den3[rows4, :]
        out_f[rows_nat, :] = num / den
        return 0

    lax.fori_loop(0, seq // ct, combine, 0)
    o_ref[...] = out_f[...].astype(bf16)


def _dil(proj, batch, seq):
    m = batch * seq
    npair = D_MODEL // LANES
    assert seq == DIL_3 * WINDOW, "pattern 3 is written for one 128-token block per subsequence"
    spec = lambda col: pl.BlockSpec((seq, LANES), lambda b, p: (b, col // LANES + p))
    nh = 2 * DIL_PATTERNS
    tok_f = pltpu.VMEM((seq, LANES), f32)
    tok_b = pltpu.VMEM((seq, LANES), bf16)
    return pl.pallas_call(
        _dil_kernel,
        grid=(batch, npair),
        in_specs=[spec(COL_QC), spec(COL_KC), spec(COL_VC)],
        out_specs=pl.BlockSpec((seq, LANES), lambda b, p: (b, p)),
        out_shape=jax.ShapeDtypeStruct((m, D_MODEL), bf16),
        scratch_shapes=[tok_f] * 6 + [tok_b] * 3 + [tok_f] * 10
        + [pltpu.VMEM((2, WINDOW, 2 * WINDOW), f32), pltpu.VMEM((WINDOW, WINDOW), f32),
           pltpu.VMEM((2, nh, WINDOW, 2 * WINDOW), f32), pltpu.VMEM((2, nh, WINDOW, 2 * WINDOW), bf16)],
        compiler_params=_cparams(2),
        name="dil",
    )(proj, proj, proj)


def _layer_norm(z, g, b):
    mu = jnp.mean(z, axis=-1, keepdims=True)
    zc = z - mu
    var = jnp.mean(zc * zc, axis=-1, keepdims=True)
    return zc * lax.rsqrt(var + LN_EPS) * g + b


def _merge_kernel(x_ref, ya_ref, yb_ref, yc_ref, gates_ref, wb_ref, wo_ref, g_ref, b_ref, o_ref):
    for t in range(ROW_TM // ROW_SUB):
        rows = slice(t * ROW_SUB, (t + 1) * ROW_SUB)
        merged = None
        for n, y_ref in enumerate((ya_ref, yb_ref, yc_ref)):
            branch = jnp.dot(y_ref[rows, :], wb_ref[n], preferred_element_type=f32)
            gate = jax.nn.sigmoid(gates_ref[rows, n * D_MODEL:(n + 1) * D_MODEL].astype(f32))
            term = gate * branch
            merged = term if merged is None else merged + term
        mix = jnp.dot(merged.astype(bf16), wo_ref[...], preferred_element_type=f32)
        o_ref[rows, :] = _layer_norm(ALPHA * x_ref[rows, :] + mix, g_ref[...], b_ref[...])


def _merge(x, ya, yb, yc, proj, wb, wo, g, b):
    m = x.shape[0]
    row = lambda width, col_block: pl.BlockSpec((ROW_TM, width), lambda i: (i, col_block))
    gw = N_BRANCHES * D_MODEL
    return pl.pallas_call(
        _merge_kernel,
        grid=(m // ROW_TM,),
        in_specs=[row(D_MODEL, 0), row(D_MODEL, 0), row(D_MODEL, 0), row(D_MODEL, 0), row(gw, COL_GATES // gw),
                  _resident((N_BRANCHES, D_MODEL, D_MODEL), lambda i: (0, 0, 0)),
                  _resident((D_MODEL, D_MODEL), lambda i: (0, 0)),
                  _resident((1, D_MODEL), lambda i: (0, 0)),
                  _resident((1, D_MODEL), lambda i: (0, 0))],
        out_specs=row(D_MODEL, 0),
        out_shape=jax.ShapeDtypeStruct((m, D_MODEL), f32),
        compiler_params=_cparams(1),
        name="merge",
    )(x, ya, yb, yc, proj, wb, wo, g, b)


def _ffn_kernel(x_ref, win_ref, wout_ref, g_ref, b_ref, o_ref):
    for t in range(ROW_TM // ROW_SUB):
        rows = slice(t * ROW_SUB, (t + 1) * ROW_SUB)
        x = x_ref[rows, :]
        xb = x.astype(bf16)
        acc = None
        for c in range(FF_HIDDEN // FFN_CH):
            h1 = jnp.dot(xb, win_ref[:, c * FFN_CH:(c + 1) * FFN_CH], preferred_element_type=f32)
            h3 = jnp.dot(xb, win_ref[:, FF_HIDDEN + c * FFN_CH:FF_HIDDEN + (c + 1) * FFN_CH],
                         preferred_element_type=f32)
            act = (jax.nn.silu(h1) * h3).astype(bf16)
            part = jnp.dot(act, wout_ref[c * FFN_CH:(c + 1) * FFN_CH, :], preferred_element_type=f32)
            acc = part if acc is None else acc + part
        o_ref[rows, :] = _layer_norm(ALPHA * x + acc, g_ref[...], b_ref[...])


def _ffn(x, win, wout, g, b):
    m = x.shape[0]
    return pl.pallas_call(
        _ffn_kernel,
        grid=(m // ROW_TM,),
        in_specs=[pl.BlockSpec((ROW_TM, D_MODEL), lambda i: (i, 0)),
                  _resident((D_MODEL, 2 * FF_HIDDEN), lambda i: (0, 0)),
                  _resident((FF_HIDDEN, D_MODEL), lambda i: (0, 0)),
                  _resident((1, D_MODEL), lambda i: (0, 0)),
                  _resident((1, D_MODEL), lambda i: (0, 0))],
        out_specs=pl.BlockSpec((ROW_TM, D_MODEL), lambda i: (i, 0)),
        out_shape=jax.ShapeDtypeStruct((m, D_MODEL), f32),
        compiler_params=_cparams(1),
        name="ffn",
    )(x, win, wout, g, b)


def _prep_w_in(w_in):
    scale = HEAD_DIM ** -0.5 * LOG2E
    lru_x, lru_g, qb, kb, vb, qc, kc, vc, gates = jnp.split(
        w_in, [1024, 2048, 3072, 3328, 3584, 4608, 5632, 6656], axis=-1)
    w = jnp.concatenate([lru_x, lru_g, gates, qb * scale, qc * scale, kc, vc, kb, vb], axis=-1)
    return w.astype(bf16)


def _block_diag(w):
    per = LRU_CW // HEAD_DIM
    w = w.reshape(DEPTH, LRU_BLOCKS // per, per, HEAD_DIM, HEAD_DIM)
    eye = jnp.eye(per, dtype=w.dtype)
    bd = jnp.einsum("lgaij,ab->lgaibj", w, eye)
    return bd.reshape(DEPTH, LRU_BLOCKS // per, LRU_CW, LRU_CW).astype(bf16)


def kernel(x, w_in, conv_w, conv_b, w_rg, b_rg, w_ig, b_ig, lru_lambda, sinks, w_branch, w_out,
           ln1_g, ln1_b, w_ffn_in, w_ffn_out, ln2_g, ln2_b):
    batch, seq, d = x.shape
    assert d == D_MODEL and w_in.shape == (DEPTH, D_MODEL, IN_WIDTH)
    w_in_b = _prep_w_in(w_in)
    wr_b = _block_diag(w_rg)
    wi_b = _block_diag(w_ig)
    wb_b = w_branch.astype(bf16)
    wo_b = w_out.astype(bf16)
    wfi_b = w_ffn_in.astype(bf16)
    wfo_b = w_ffn_out.astype(bf16)
    row = lambda p, l: p[l].reshape(1, D_MODEL)

    h = x.reshape(batch * seq, D_MODEL)
    for l in range(DEPTH):
        proj, ya = _proj_lru(h, w_in_b[l], conv_w[l], row(conv_b, l), wr_b[l], row(b_rg, l), wi_b[l],
                             row(b_ig, l), row(lru_lambda, l), seq)
        yb = _swa(proj, sinks[l], batch, seq)
        yc = _dil(proj, batch, seq)
        h = _merge(h, ya, yb, yc, proj, wb_b[l], wo_b[l], row(ln1_g, l), row(ln1_b, l))
        h = _ffn(h, wfi_b[l], wfo_b[l], row(ln2_g, l), row(ln2_b, l))
    return h.reshape(batch, seq, D_MODEL)
```

```python
import functools

import jax
import jax.numpy as jnp
from jax import lax
from jax.experimental import pallas as pl
from jax.experimental.pallas import tpu as pltpu

f32 = jnp.float32
bf16 = jnp.bfloat16

D_MODEL = 1024
DEPTH = 4
HEAD_DIM = 64
N_BRANCHES = 3
LRU_BLOCKS = 16
CONV_WIDTH = 4
LRU_C = 8.0
WINDOW = 128
DIL_2, DIL_3 = 4, 16
FF_HIDDEN = 2816
ALPHA = (2.0 * DEPTH) ** 0.25
LN_EPS = 1e-5
NEG_INF = -1e30
LOG2E = 1.4426950408889634

LANES = 128
SUBLANES = 8
VMEM_LIMIT = 56 * 1024 * 1024

IN_WIDTH = 9728
LRU_COLS = 2 * D_MODEL
REST_WIDTH = IN_WIDTH - LRU_COLS
COL_GATES = 0
COL_QB = 3072
COL_QC = 4096
COL_KC = 5120
COL_VC = 6144
COL_KB = 7168
COL_VB = 7424

PROJ_TM = 512
PROJ_TN = 512
LRU_CW = 256
LRU_T = 128
ROW_TM = 1024
ROW_SUB = 512
FFN_CH = 256
SWA_PAIRS = 4
DIL_PATTERNS = 3
SWA_STEPS = 4
DIL_STEPS = 4


def _cparams(n_grid):
    return pltpu.CompilerParams(dimension_semantics=("arbitrary",) * n_grid, vmem_limit_bytes=VMEM_LIMIT)


def _resident(block_shape, index_map):
    return pl.BlockSpec(block_shape, index_map, pipeline_mode=pl.Buffered(1))


def _conv(xe, cw, cb):
    xc = cb + cw[CONV_WIDTH - 1:CONV_WIDTH] * xe[SUBLANES:]
    for s in range(1, CONV_WIDTH):
        xc = xc + cw[CONV_WIDTH - 1 - s:CONV_WIDTH - s] * pltpu.roll(xe, s, 0)[SUBLANES:]
    return xc


def _lru_rows(xc, r_pre, i_pre, gate, carry, br, bi, sp):
    groups = LRU_T // SUBLANES
    row8 = lax.broadcasted_iota(jnp.int32, (groups, SUBLANES, LRU_CW), 1)
    r = jax.nn.sigmoid(r_pre + br)
    i = jax.nn.sigmoid(i_pre + bi)
    log_a = (-LRU_C * r) * sp
    a = jnp.exp(log_a)
    mult = jnp.sqrt(1.0 - a * a)
    b = mult * (i * xc)
    a3 = a.reshape(groups, SUBLANES, LRU_CW)
    b3 = b.reshape(groups, SUBLANES, LRU_CW)
    for d in (1, 2, 4):
        a_s = jnp.where(row8 >= d, pltpu.roll(a3, d, 1), 1.0)
        b_s = jnp.where(row8 >= d, pltpu.roll(b3, d, 1), 0.0)
        b3 = a3 * b_s + b3
        a3 = a3 * a_s
    hs = []
    for g in range(groups):
        h8 = a3[g] * carry + b3[g]
        carry = h8[SUBLANES - 1:SUBLANES, :]
        hs.append(h8)
    return (jnp.concatenate(hs, axis=0) * jax.nn.gelu(gate)).astype(bf16), carry


def _proj_lru_kernel(tiles_per_seq, x_ref, w_ref, cw_ref, cb_ref, wr_ref, br_ref, wi_ref, bi_ref, lam_ref,
                     rest_ref, ya_ref, lx_ref, gt_ref, carry_ref):
    i = pl.program_id(0)
    slot = i % 2
    rslot = 1 - slot
    n_rc = PROJ_TM // LRU_T
    n_cg = D_MODEL // LRU_CW
    n_rest = REST_WIDTH // PROJ_TN
    assert n_rc * n_cg == n_rest + 1

    @pl.when(i == 0)
    def _():
        lx_ref[...] = jnp.zeros(lx_ref.shape, f32)
        gt_ref[...] = jnp.zeros(gt_ref.shape, f32)
        carry_ref[...] = jnp.zeros(carry_ref.shape, f32)

    prev_starts_seq = (i - 1) % tiles_per_seq == 0
    this_starts_seq = i % tiles_per_seq == 0
    xb = x_ref[...].astype(bf16)

    def lru_group(cg):
        lanes = slice(cg * LRU_CW, (cg + 1) * LRU_CW)
        xc = _conv(lx_ref[rslot, :, lanes], cw_ref[:, lanes], cb_ref[:, lanes])
        xcb = xc.astype(bf16)
        r_pre = jnp.dot(xcb, wr_ref[cg], preferred_element_type=f32)
        i_pre = jnp.dot(xcb, wi_ref[cg], preferred_element_type=f32)
        sp = jax.nn.softplus(-lam_ref[:, lanes])
        state = {"carry": jnp.where(prev_starts_seq, 0.0, carry_ref[:, lanes])}

        def finish(rc):
            rows = slice(rc * LRU_T, (rc + 1) * LRU_T)
            y, state["carry"] = _lru_rows(xc[rows], r_pre[rows], i_pre[rows], gt_ref[rslot, rows, lanes],
                                          state["carry"], br_ref[:, lanes], bi_ref[:, lanes], sp)
            ya_ref[rows, lanes] = y
            if rc == n_rc - 1:
                carry_ref[:, lanes] = state["carry"]

        return finish

    finish = lru_group(0)
    lx_ref[slot, SUBLANES:, :] = jnp.dot(xb, w_ref[:, 0:D_MODEL], preferred_element_type=f32)
    gt_ref[slot] = jnp.dot(xb, w_ref[:, D_MODEL:LRU_COLS], preferred_element_type=f32)
    tail = lx_ref[rslot, PROJ_TM:PROJ_TM + SUBLANES, :]
    lx_ref[slot, 0:SUBLANES, :] = jnp.where(this_starts_seq, 0.0, tail)

    for c in range(n_rc * n_cg):
        cg, rc = c // n_rc, c % n_rc
        if rc == 0 and cg > 0:
            finish = lru_group(cg)
        if c < n_rest:
            cols = slice(c * PROJ_TN, (c + 1) * PROJ_TN)
            acc = jnp.dot(xb, w_ref[:, LRU_COLS + c * PROJ_TN:LRU_COLS + (c + 1) * PROJ_TN],
                          preferred_element_type=f32)
            rest_ref[:, cols] = acc.astype(bf16)
        finish(rc)


def _proj_lru(x, w, cw, cb, wr, br, wi, bi, lam, seq, layer):
    m = x.shape[0]
    nt = m // PROJ_TM
    last = nt - 1
    vec = lambda rows: _resident((rows, D_MODEL), lambda i: (0, 0))
    mat = _resident((None, D_MODEL // LRU_CW, LRU_CW, LRU_CW), lambda i: (layer, 0, 0, 0))
    return pl.pallas_call(
        functools.partial(_proj_lru_kernel, seq // PROJ_TM),
        grid=(nt + 1,),
        in_specs=[pl.BlockSpec((PROJ_TM, D_MODEL), lambda i: (jnp.minimum(i, last), 0)),
                  _resident((None, D_MODEL, IN_WIDTH), lambda i: (layer, 0, 0)),
                  vec(CONV_WIDTH), vec(1), mat, vec(1), mat, vec(1), vec(1)],
        out_specs=[pl.BlockSpec((PROJ_TM, REST_WIDTH), lambda i: (jnp.minimum(i, last), 0)),
                   pl.BlockSpec((PROJ_TM, D_MODEL), lambda i: (jnp.maximum(i - 1, 0), 0))],
        out_shape=[jax.ShapeDtypeStruct((m, REST_WIDTH), bf16), jax.ShapeDtypeStruct((m, D_MODEL), bf16)],
        scratch_shapes=[pltpu.VMEM((2, PROJ_TM + SUBLANES, D_MODEL), f32),
                        pltpu.VMEM((2, PROJ_TM, D_MODEL), f32),
                        pltpu.VMEM((1, D_MODEL), f32)],
        compiler_params=_cparams(1),
        name="proj_lru",
    )(x, w, cw, cb, wr, br, wi, bi, lam)


def _band_bias(first_block):
    qi = lax.broadcasted_iota(jnp.int32, (WINDOW, 2 * WINDOW), 0)
    kj = lax.broadcasted_iota(jnp.int32, (WINDOW, 2 * WINDOW), 1)
    rel = qi + WINDOW - kj
    ok = jnp.logical_and(rel >= 0, rel <= WINDOW)
    if first_block:
        ok = jnp.logical_and(ok, kj >= WINDOW)
    return jnp.where(ok, 0.0, NEG_INF).astype(f32)


def _causal_bias():
    qi = lax.broadcasted_iota(jnp.int32, (WINDOW, WINDOW), 0)
    kj = lax.broadcasted_iota(jnp.int32, (WINDOW, WINDOW), 1)
    return jnp.where(qi >= kj, 0.0, NEG_INF).astype(f32)


def _nt_dot(a, b):
    return lax.dot_general(a, b, (((1,), (1,)), ((), ())), preferred_element_type=f32)


def _lane_lo():
    return lax.broadcasted_iota(jnp.int32, (1, LANES), 1) < HEAD_DIM


def _per_head(x, fill):
    lo = _lane_lo()
    f = jnp.full_like(x, fill)
    return jnp.where(lo, x, f), jnp.where(lo, f, x)


def _join_pair(a0, a1):
    lo = _lane_lo()
    return jnp.where(lo, a0, a1), pltpu.roll(jnp.where(lo, a1, a0), HEAD_DIM, 1)


def _run_pipeline(nblk, qk, softmax, pv, steps_per_trip):
    assert steps_per_trip % 2 == 0 and nblk % steps_per_trip == 0 and nblk >= 2 * steps_per_trip

    def step(t, slot):
        pv(t - 2, slot)
        qk(t, slot)
        softmax(t - 1, 1 - slot)

    qk(0, 0)
    qk(1, 1)
    softmax(0, 0)
    for t in range(2, steps_per_trip):
        step(t, t % 2)

    def trip(i, _):
        for u in range(steps_per_trip):
            step(steps_per_trip * i + u, u % 2)
        return 0

    lax.fori_loop(1, nblk // steps_per_trip, trip, 0)
    pv(nblk - 2, 0)
    softmax(nblk - 1, 1)
    pv(nblk - 1, 1)


def _block_rows(j):
    if isinstance(j, int):
        return pl.ds(max(j - 1, 0) * WINDOW, WINDOW), pl.ds(j * WINDOW, WINDOW)
    cur = pl.multiple_of(j * WINDOW, WINDOW)
    prev = pl.multiple_of(jnp.maximum(j - 1, 0) * WINDOW, WINDOW)
    return pl.ds(prev, WINDOW), pl.ds(cur, WINDOW)


def _swa_kernel(sink_ref, q_ref, k_ref, v_ref, o_ref, bias_ref, s_buf, p_buf, e_buf, kd_ref, vd_ref):
    seq = q_ref.shape[0]
    kvp = pl.program_id(1)
    lo = _lane_lo()

    @pl.when(jnp.logical_and(pl.program_id(0) == 0, kvp == 0))
    def _():
        bias_ref[0] = _band_bias(False)
        bias_ref[1] = _band_bias(True)

    lane_rolled_k = pltpu.roll(k_ref[...], HEAD_DIM, 1)
    lane_rolled_v = pltpu.roll(v_ref[...], HEAD_DIM, 1)
    for kvh in (0, 1):
        sel = lo if kvh == 0 else jnp.logical_not(lo)
        kd_ref[kvh] = jnp.where(sel, k_ref[...], lane_rolled_k)
        vd_ref[kvh] = jnp.where(sel, v_ref[...], lane_rolled_v)

    for kvh in (0, 1):
        pairs = (2 * kvh, 2 * kvh + 1)

        def kv_blocks(ref, j, fill, kvh=kvh):
            prev, cur = _block_rows(j)
            return _per_head(jnp.concatenate([ref[kvh, prev, :], ref[kvh, cur, :]], axis=0), fill)

        def qk(j, slot, pairs=pairs, kv_blocks=kv_blocks):
            _, cur = _block_rows(j)
            bias = bias_ref[jnp.where(j == 0, 1, 0)]
            ks = kv_blocks(kd_ref, j, 0)
            for pair in pairs:
                q2 = q_ref[cur, pair * LANES:(pair + 1) * LANES]
                for h in (0, 1):
                    s_buf[slot, 2 * pair + h] = _nt_dot(q2, ks[h]) + bias

        def softmax(j, slot, pairs=pairs):
            for pair in pairs:
                sink_terms = []
                for h in (0, 1):
                    sink = sink_ref[kvp * 2 * SWA_PAIRS + 2 * pair + h] * LOG2E
                    s = s_buf[slot, 2 * pair + h]
                    m = jnp.maximum(jnp.max(s, axis=-1, keepdims=True), sink)
                    p_buf[slot, 2 * pair + h] = jnp.exp2(s - m).astype(bf16)
                    sink_terms.append(jnp.exp2(sink - m))
                e_buf[slot, pair] = jnp.where(lo, sink_terms[0], sink_terms[1])

        def pv(j, slot, pairs=pairs, kv_blocks=kv_blocks):
            _, cur = _block_rows(j)
            vs = kv_blocks(vd_ref, j, 1)
            for pair in pairs:
                accs = [jnp.dot(p_buf[slot, 2 * pair + h], vs[h], preferred_element_type=f32) for h in (0, 1)]
                num, den = _join_pair(*accs)
                den = den + e_buf[slot, pair]
                o_ref[cur, pair * LANES:(pair + 1) * LANES] = (num / den).astype(bf16)

        _run_pipeline(seq // WINDOW, qk, softmax, pv, SWA_STEPS)


def _swa(proj, sinks, batch, seq):
    m = batch * seq
    qw = SWA_PAIRS * LANES
    nh = 2 * SWA_PAIRS
    return pl.pallas_call(
        _swa_kernel,
        grid=(batch, 2),
        in_specs=[pl.BlockSpec(memory_space=pltpu.SMEM),
                  pl.BlockSpec((seq, qw), lambda b, p: (b, COL_QB // qw + p)),
                  pl.BlockSpec((seq, LANES), lambda b, p: (b, COL_KB // LANES + p)),
                  pl.BlockSpec((seq, LANES), lambda b, p: (b, COL_VB // LANES + p))],
        out_specs=pl.BlockSpec((seq, qw), lambda b, p: (b, p)),
        out_shape=jax.ShapeDtypeStruct((m, D_MODEL), bf16),
        scratch_shapes=[pltpu.VMEM((2, WINDOW, 2 * WINDOW), f32),
                        pltpu.VMEM((2, nh, WINDOW, 2 * WINDOW), f32),
                        pltpu.VMEM((2, nh, WINDOW, 2 * WINDOW), bf16),
                        pltpu.VMEM((2, SWA_PAIRS, WINDOW, LANES), f32),
                        pltpu.VMEM((2, seq, LANES), bf16), pltpu.VMEM((2, seq, LANES), bf16)],
        compiler_params=_cparams(2),
        name="swa",
    )(sinks, proj, proj, proj)


def _dil_kernel(q_ref, k_ref, v_ref, o_ref, qf, kf, vf, q4, k4, v4, q4b, k4b, v4b,
                num1, den1, m1, num2, den2, m2, num3, den3, m3, out_f, bias_ref, causal_ref, s_buf, p_buf):
    seq = q_ref.shape[0]
    lo = _lane_lo()
    sub = seq // DIL_2
    nb2 = sub // WINDOW

    @pl.when(jnp.logical_and(pl.program_id(0) == 0, pl.program_id(1) == 0))
    def _():
        bias_ref[0] = _band_bias(False)
        bias_ref[1] = _band_bias(True)
        causal_ref[...] = _causal_bias()

    for src, nat, four, four_b in ((q_ref, qf, q4, q4b), (k_ref, kf, k4, k4b), (v_ref, vf, v4, v4b)):
        nat[...] = src[...].astype(f32)
        for r in range(DIL_2):
            part = nat[pl.ds(r, sub, stride=DIL_2), :]
            four[r * sub:(r + 1) * sub, :] = part
            four_b[r * sub:(r + 1) * sub, :] = part.astype(bf16)

    def rows2(j):
        r, jb = j // nb2, j % nb2
        cur = pl.multiple_of(r * sub + jb * WINDOW, WINDOW)
        prev = pl.multiple_of(r * sub + jnp.maximum(jb - 1, 0) * WINDOW, WINDOW)
        return pl.ds(prev, WINDOW), pl.ds(cur, WINDOW), jb == 0

    def rows3(j):
        return pl.ds((j % DIL_2) * sub + j // DIL_2, WINDOW, stride=DIL_2)

    def gather_kv(ref, ref4, ref4b, j):
        prev1, cur1 = _block_rows(j)
        prev2, cur2, _ = rows2(j)
        return (jnp.concatenate([ref[prev1, :], ref[cur1, :]], axis=0),
                jnp.concatenate([ref4b[prev2, :], ref4b[cur2, :]], axis=0),
                ref4[rows3(j), :].astype(bf16))

    def qk(j, slot):
        _, cur1 = _block_rows(j)
        _, cur2, first2 = rows2(j)
        qs = (q_ref[cur1, :], q4b[cur2, :], q4[rows3(j), :].astype(bf16))
        ks = gather_kv(k_ref, k4, k4b, j)
        biases = (bias_ref[jnp.where(j == 0, 1, 0)], bias_ref[jnp.where(first2, 1, 0)], causal_ref[...])
        for c in range(DIL_PATTERNS):
            for h, k_h in enumerate(_per_head(ks[c], 0)):
                s = _nt_dot(qs[c], k_h) + biases[c]
                s_buf[slot, 2 * c + h, :, 0:s.shape[1]] = s

    def softmax(j, slot):
        _, cur1 = _block_rows(j)
        _, cur2, _ = rows2(j)
        for c, (m_ref, rows) in enumerate(((m1, cur1), (m2, cur2), (m3, rows3(j)))):
            nk = WINDOW if c == 2 else 2 * WINDOW
            ms = []
            for h in (0, 1):
                s = s_buf[slot, 2 * c + h, :, 0:nk]
                m = jnp.max(s, axis=-1, keepdims=True)
                p_buf[slot, 2 * c + h, :, 0:nk] = jnp.exp2(s - m).astype(bf16)
                ms.append(m)
            m_ref[rows, :] = jnp.where(lo, ms[0], ms[1])

    def pv(j, slot):
        _, cur1 = _block_rows(j)
        _, cur2, _ = rows2(j)
        vs = gather_kv(v_ref, v4, v4b, j)
        dests = ((num1, den1, cur1), (num2, den2, cur2), (num3, den3, rows3(j)))
        for c, (num_ref, den_ref, rows) in enumerate(dests):
            nk = WINDOW if c == 2 else 2 * WINDOW
            accs = [jnp.dot(p_buf[slot, 2 * c + h, :, 0:nk], v_h, preferred_element_type=f32)
                    for h, v_h in enumerate(_per_head(vs[c], 1))]
            num, den = _join_pair(*accs)
            num_ref[rows, :] = num
            den_ref[rows, :] = den

    _run_pipeline(seq // WINDOW, qk, softmax, pv, DIL_STEPS)

    ct = 256
    per_res = sub // ct

    def combine(c, _):
        r, part = c // per_res, c % per_res
        rows4 = pl.ds(pl.multiple_of(c * ct, ct), ct)
        rows_nat = pl.ds(r + DIL_2 * ct * part, ct, stride=DIL_2)
        ma, mb, mc = m1[rows_nat, :], m2[rows4, :], m3[rows4, :]
        mx = jnp.maximum(jnp.maximum(ma, mb), mc)
        wa, wb, wc = jnp.exp2(ma - mx), jnp.exp2(mb - mx), jnp.exp2(mc - mx)
        num = wa * num1[rows_nat, :] + wb * num2[rows4, :] + wc * num3[rows4, :]
        den = wa * den1[rows_nat, :] + wb * den2[rows4, :] + wc * den3[rows4, :]
        out_f[rows_nat, :] = num / den
        return 0

    lax.fori_loop(0, seq // ct, combine, 0)
    o_ref[...] = out_f[...].astype(bf16)


def _dil(proj, batch, seq):
    m = batch * seq
    npair = D_MODEL // LANES
    assert seq == DIL_3 * WINDOW, "pattern 3 is written for one 128-token block per subsequence"
    spec = lambda col: pl.BlockSpec((seq, LANES), lambda b, p: (b, col // LANES + p))
    nh = 2 * DIL_PATTERNS
    tok_f = pltpu.VMEM((seq, LANES), f32)
    tok_b = pltpu.VMEM((seq, LANES), bf16)
    return pl.pallas_call(
        _dil_kernel,
        grid=(batch, npair),
        in_specs=[spec(COL_QC), spec(COL_KC), spec(COL_VC)],
        out_specs=pl.BlockSpec((seq, LANES), lambda b, p: (b, p)),
        out_shape=jax.ShapeDtypeStruct((m, D_MODEL), bf16),
        scratch_shapes=[tok_f] * 6 + [tok_b] * 3 + [tok_f] * 10
        + [pltpu.VMEM((2, WINDOW, 2 * WINDOW), f32), pltpu.VMEM((WINDOW, WINDOW), f32),
           pltpu.VMEM((2, nh, WINDOW, 2 * WINDOW), f32), pltpu.VMEM((2, nh, WINDOW, 2 * WINDOW), bf16)],
        compiler_params=_cparams(2),
        name="dil",
    )(proj, proj, proj)


def _layer_norm(z, g, b):
    mu = jnp.mean(z, axis=-1, keepdims=True)
    zc = z - mu
    var = jnp.mean(zc * zc, axis=-1, keepdims=True)
    return zc * lax.rsqrt(var + LN_EPS) * g + b


def _merge_kernel(x_ref, ya_ref, yb_ref, yc_ref, gates_ref, wb_ref, wo_ref, g_ref, b_ref, o_ref):
    for t in range(ROW_TM // ROW_SUB):
        rows = slice(t * ROW_SUB, (t + 1) * ROW_SUB)
        merged = None
        for n, y_ref in enumerate((ya_ref, yb_ref, yc_ref)):
            branch = jnp.dot(y_ref[rows, :], wb_ref[n], preferred_element_type=f32)
            gate = jax.nn.sigmoid(gates_ref[rows, n * D_MODEL:(n + 1) * D_MODEL].astype(f32))
            term = gate * branch
            merged = term if merged is None else merged + term
        mix = jnp.dot(merged.astype(bf16), wo_ref[...], preferred_element_type=f32)
        o_ref[rows, :] = _layer_norm(ALPHA * x_ref[rows, :] + mix, g_ref[...], b_ref[...])


def _merge(x, ya, yb, yc, proj, wb, wo, g, b, layer):
    m = x.shape[0]
    row = lambda width, col_block: pl.BlockSpec((ROW_TM, width), lambda i: (i, col_block))
    gw = N_BRANCHES * D_MODEL
    return pl.pallas_call(
        _merge_kernel,
        grid=(m // ROW_TM,),
        in_specs=[row(D_MODEL, 0), row(D_MODEL, 0), row(D_MODEL, 0), row(D_MODEL, 0), row(gw, COL_GATES // gw),
                  _resident((None, N_BRANCHES, D_MODEL, D_MODEL), lambda i: (layer, 0, 0, 0)),
                  _resident((None, D_MODEL, D_MODEL), lambda i: (layer, 0, 0)),
                  _resident((1, D_MODEL), lambda i: (0, 0)),
                  _resident((1, D_MODEL), lambda i: (0, 0))],
        out_specs=row(D_MODEL, 0),
        out_shape=jax.ShapeDtypeStruct((m, D_MODEL), f32),
        compiler_params=_cparams(1),
        name="merge",
    )(x, ya, yb, yc, proj, wb, wo, g, b)


def _ffn_kernel(x_ref, win_ref, wout_ref, g_ref, b_ref, o_ref):
    for t in range(ROW_TM // ROW_SUB):
        rows = slice(t * ROW_SUB, (t + 1) * ROW_SUB)
        x = x_ref[rows, :]
        xb = x.astype(bf16)
        acc = None
        for c in range(FF_HIDDEN // FFN_CH):
            h1 = jnp.dot(xb, win_ref[:, c * FFN_CH:(c + 1) * FFN_CH], preferred_element_type=f32)
            h3 = jnp.dot(xb, win_ref[:, FF_HIDDEN + c * FFN_CH:FF_HIDDEN + (c + 1) * FFN_CH],
                         preferred_element_type=f32)
            act = (jax.nn.silu(h1) * h3).astype(bf16)
            part = jnp.dot(act, wout_ref[c * FFN_CH:(c + 1) * FFN_CH, :], preferred_element_type=f32)
            acc = part if acc is None else acc + part
        o_ref[rows, :] = _layer_norm(ALPHA * x + acc, g_ref[...], b_ref[...])


def _ffn(x, win, wout, g, b, layer):
    m = x.shape[0]
    return pl.pallas_call(
        _ffn_kernel,
        grid=(m // ROW_TM,),
        in_specs=[pl.BlockSpec((ROW_TM, D_MODEL), lambda i: (i, 0)),
                  _resident((None, D_MODEL, 2 * FF_HIDDEN), lambda i: (layer, 0, 0)),
                  _resident((None, FF_HIDDEN, D_MODEL), lambda i: (layer, 0, 0)),
                  _resident((1, D_MODEL), lambda i: (0, 0)),
                  _resident((1, D_MODEL), lambda i: (0, 0))],
        out_specs=pl.BlockSpec((ROW_TM, D_MODEL), lambda i: (i, 0)),
        out_shape=jax.ShapeDtypeStruct((m, D_MODEL), f32),
        compiler_params=_cparams(1),
        name="ffn",
    )(x, win, wout, g, b)


def _prep_w_in(w_in):
    scale = HEAD_DIM ** -0.5 * LOG2E
    lru_x, lru_g, qb, kb, vb, qc, kc, vc, gates = jnp.split(
        w_in, [1024, 2048, 3072, 3328, 3584, 4608, 5632, 6656], axis=-1)
    pieces = [lru_x, lru_g, gates, qb * scale, qc * scale, kc, vc, kb, vb]
    return jnp.concatenate([p.astype(bf16) for p in pieces], axis=-1)


def _block_diag(w):
    per = LRU_CW // HEAD_DIM
    w = w.reshape(DEPTH, LRU_BLOCKS // per, per, HEAD_DIM, HEAD_DIM)
    eye = jnp.eye(per, dtype=w.dtype)
    bd = jnp.einsum("lgaij,ab->lgaibj", w, eye)
    return bd.reshape(DEPTH, LRU_BLOCKS // per, LRU_CW, LRU_CW).astype(bf16)


def kernel(x, w_in, conv_w, conv_b, w_rg, b_rg, w_ig, b_ig, lru_lambda, sinks, w_branch, w_out,
           ln1_g, ln1_b, w_ffn_in, w_ffn_out, ln2_g, ln2_b):
    batch, seq, d = x.shape
    assert d == D_MODEL and w_in.shape == (DEPTH, D_MODEL, IN_WIDTH)
    w_in_b = _prep_w_in(w_in)
    wr_b = _block_diag(w_rg)
    wi_b = _block_diag(w_ig)
    wb_b = w_branch.astype(bf16)
    wo_b = w_out.astype(bf16)
    wfi_b = w_ffn_in.astype(bf16)
    wfo_b = w_ffn_out.astype(bf16)
    row = lambda p, l: p[l].reshape(1, D_MODEL)

    h = x.reshape(batch * seq, D_MODEL)
    for l in range(DEPTH):
        proj, ya = _proj_lru(h, w_in_b, conv_w[l], row(conv_b, l), wr_b, row(b_rg, l), wi_b,
                             row(b_ig, l), row(lru_lambda, l), seq, l)
        yb = _swa(proj, sinks[l], batch, seq)
        yc = _dil(proj, batch, seq)
        h = _merge(h, ya, yb, yc, proj, wb_b, wo_b, row(ln1_g, l), row(ln1_b, l), l)
        h = _ffn(h, wfi_b, wfo_b, row(ln2_g, l), row(ln2_b, l), l)
    return h.reshape(batch, seq, D_MODEL)
```

```python
import functools

import jax
import jax.numpy as jnp
from jax import lax
from jax.experimental import pallas as pl
from jax.experimental.pallas import tpu as pltpu

f32 = jnp.float32
bf16 = jnp.bfloat16

D_MODEL = 1024
DEPTH = 4
HEAD_DIM = 64
N_BRANCHES = 3
LRU_BLOCKS = 16
CONV_WIDTH = 4
LRU_C = 8.0
WINDOW = 128
DIL_2, DIL_3 = 4, 16
FF_HIDDEN = 2816
ALPHA = (2.0 * DEPTH) ** 0.25
LN_EPS = 1e-5
NEG_INF = -1e30
LOG2E = 1.4426950408889634

LANES = 128
SUBLANES = 8
VMEM_LIMIT = 56 * 1024 * 1024

IN_WIDTH = 9728
LRU_COLS = 2 * D_MODEL
REST_WIDTH = IN_WIDTH - LRU_COLS
COL_GATES = 0
COL_QB = 3072
COL_QC = 4096
COL_KC = 5120
COL_VC = 6144
COL_KB = 7168
COL_VB = 7424

PROJ_TM = 512
PROJ_TN = 512
LRU_CW = 256
LRU_T = 128
ROW_TM = 1024
ROW_SUB = 512
FFN_CH = 256
SWA_PAIRS = 4
DIL_PATTERNS = 3
SWA_STEPS = 4
DIL_STEPS = 4


def _cparams(n_grid):
    return pltpu.CompilerParams(dimension_semantics=("arbitrary",) * n_grid, vmem_limit_bytes=VMEM_LIMIT)


def _resident(block_shape, index_map):
    return pl.BlockSpec(block_shape, index_map, pipeline_mode=pl.Buffered(1))


def _conv(xe, cw, cb):
    xc = cb + cw[CONV_WIDTH - 1:CONV_WIDTH] * xe[SUBLANES:]
    for s in range(1, CONV_WIDTH):
        xc = xc + cw[CONV_WIDTH - 1 - s:CONV_WIDTH - s] * pltpu.roll(xe, s, 0)[SUBLANES:]
    return xc


SCAN_ROWS = 4


def _lru_rows(xc, r_pre, i_pre, gate, carry, br, bi, sp, scan_ref):
    rows, width = xc.shape
    r = jax.nn.sigmoid(r_pre + br)
    i = jax.nn.sigmoid(i_pre + bi)
    log_a = (-LRU_C * r) * sp
    a = jnp.exp(log_a)
    mult = jnp.sqrt(1.0 - a * a)
    b = mult * (i * xc)
    a_ref, b_ref, h_ref = scan_ref.at[0], scan_ref.at[1], scan_ref.at[2]
    block = SCAN_ROWS * SUBLANES
    row = lax.broadcasted_iota(jnp.int32, (SUBLANES, LANES), 0)
    carries = []
    for lc in range(width // LANES):
        lanes = slice(lc * LANES, (lc + 1) * LANES)
        a_ref[lc] = a[:, lanes]
        b_ref[lc] = b[:, lanes]
        c_in = carry[:, lanes]
        for blk in range(rows // block):
            run = lambda ref, j: ref[lc, pl.ds(blk * block + j, SUBLANES, stride=SCAN_ROWS), :]
            hs, ps = [run(b_ref, 0)], [run(a_ref, 0)]
            for j in range(1, SCAN_ROWS):
                a_j = run(a_ref, j)
                hs.append(a_j * hs[-1] + run(b_ref, j))
                ps.append(a_j * ps[-1])
            q, e = ps[-1], hs[-1]
            for d in (1, 2, 4):
                q_s = jnp.where(row >= d, pltpu.roll(q, d, 0), 1.0)
                e_s = jnp.where(row >= d, pltpu.roll(e, d, 0), 0.0)
                e = q * e_s + e
                q = q * q_s
            after = q * c_in + e
            before = jnp.where(row == 0, c_in, pltpu.roll(after, 1, 0))
            c_in = after[SUBLANES - 1:SUBLANES, :]
            for j in range(SCAN_ROWS):
                h_ref[lc, pl.ds(blk * block + j, SUBLANES, stride=SCAN_ROWS), :] = hs[j] + ps[j] * before
        carries.append(c_in)
    h = jnp.concatenate([h_ref[lc] for lc in range(width // LANES)], axis=1)
    return (h * jax.nn.gelu(gate)).astype(bf16), jnp.concatenate(carries, axis=1)


def _proj_lru_kernel(tiles_per_seq, x_ref, w_ref, cw_ref, cb_ref, wr_ref, br_ref, wi_ref, bi_ref, lam_ref,
                     rest_ref, ya_ref, lx_ref, gt_ref, carry_ref, scan_ref):
    i = pl.program_id(0)
    slot = i % 2
    rslot = 1 - slot
    n_rc = PROJ_TM // LRU_T
    n_cg = D_MODEL // LRU_CW
    n_rest = REST_WIDTH // PROJ_TN
    assert n_rc * n_cg == n_rest + 1

    @pl.when(i == 0)
    def _():
        lx_ref[...] = jnp.zeros(lx_ref.shape, f32)
        gt_ref[...] = jnp.zeros(gt_ref.shape, f32)
        carry_ref[...] = jnp.zeros(carry_ref.shape, f32)

    prev_starts_seq = (i - 1) % tiles_per_seq == 0
    this_starts_seq = i % tiles_per_seq == 0
    xb = x_ref[...].astype(bf16)

    def lru_group(cg):
        lanes = slice(cg * LRU_CW, (cg + 1) * LRU_CW)
        xc = _conv(lx_ref[rslot, :, lanes], cw_ref[:, lanes], cb_ref[:, lanes])
        xcb = xc.astype(bf16)
        r_pre = jnp.dot(xcb, wr_ref[cg], preferred_element_type=f32)
        i_pre = jnp.dot(xcb, wi_ref[cg], preferred_element_type=f32)
        sp = jax.nn.softplus(-lam_ref[:, lanes])
        state = {"carry": jnp.where(prev_starts_seq, 0.0, carry_ref[:, lanes])}

        def finish(rc):
            rows = slice(rc * LRU_T, (rc + 1) * LRU_T)
            y, state["carry"] = _lru_rows(xc[rows], r_pre[rows], i_pre[rows], gt_ref[rslot, rows, lanes],
                                          state["carry"], br_ref[:, lanes], bi_ref[:, lanes], sp,
                                          scan_ref.at[rc % 2])
            ya_ref[rows, lanes] = y
            if rc == n_rc - 1:
                carry_ref[:, lanes] = state["carry"]

        return finish

    finish = lru_group(0)
    lx_ref[slot, SUBLANES:, :] = jnp.dot(xb, w_ref[:, 0:D_MODEL], preferred_element_type=f32)
    gt_ref[slot] = jnp.dot(xb, w_ref[:, D_MODEL:LRU_COLS], preferred_element_type=f32)
    tail = lx_ref[rslot, PROJ_TM:PROJ_TM + SUBLANES, :]
    lx_ref[slot, 0:SUBLANES, :] = jnp.where(this_starts_seq, 0.0, tail)

    for c in range(n_rc * n_cg):
        cg, rc = c // n_rc, c % n_rc
        if rc == 0 and cg > 0:
            finish = lru_group(cg)
        if c < n_rest:
            cols = slice(c * PROJ_TN, (c + 1) * PROJ_TN)
            acc = jnp.dot(xb, w_ref[:, LRU_COLS + c * PROJ_TN:LRU_COLS + (c + 1) * PROJ_TN],
                          preferred_element_type=f32)
            rest_ref[:, cols] = acc.astype(bf16)
        finish(rc)


def _proj_lru(x, w, cw, cb, wr, br, wi, bi, lam, seq, layer):
    m = x.shape[0]
    nt = m // PROJ_TM
    last = nt - 1
    vec = lambda rows: _resident((rows, D_MODEL), lambda i: (0, 0))
    mat = _resident((None, D_MODEL // LRU_CW, LRU_CW, LRU_CW), lambda i: (layer, 0, 0, 0))
    return pl.pallas_call(
        functools.partial(_proj_lru_kernel, seq // PROJ_TM),
        grid=(nt + 1,),
        in_specs=[pl.BlockSpec((PROJ_TM, D_MODEL), lambda i: (jnp.minimum(i, last), 0)),
                  _resident((None, D_MODEL, IN_WIDTH), lambda i: (layer, 0, 0)),
                  vec(CONV_WIDTH), vec(1), mat, vec(1), mat, vec(1), vec(1)],
        out_specs=[pl.BlockSpec((PROJ_TM, REST_WIDTH), lambda i: (jnp.minimum(i, last), 0)),
                   pl.BlockSpec((PROJ_TM, D_MODEL), lambda i: (jnp.maximum(i - 1, 0), 0))],
        out_shape=[jax.ShapeDtypeStruct((m, REST_WIDTH), bf16), jax.ShapeDtypeStruct((m, D_MODEL), bf16)],
        scratch_shapes=[pltpu.VMEM((2, PROJ_TM + SUBLANES, D_MODEL), f32),
                        pltpu.VMEM((2, PROJ_TM, D_MODEL), f32),
                        pltpu.VMEM((1, D_MODEL), f32),
                        pltpu.VMEM((2, 3, LRU_CW // LANES, LRU_T, LANES), f32)],
        compiler_params=_cparams(1),
        name="proj_lru",
    )(x, w, cw, cb, wr, br, wi, bi, lam)


def _band_bias(first_block):
    qi = lax.broadcasted_iota(jnp.int32, (WINDOW, 2 * WINDOW), 0)
    kj = lax.broadcasted_iota(jnp.int32, (WINDOW, 2 * WINDOW), 1)
    rel = qi + WINDOW - kj
    ok = jnp.logical_and(rel >= 0, rel <= WINDOW)
    if first_block:
        ok = jnp.logical_and(ok, kj >= WINDOW)
    return jnp.where(ok, 0.0, NEG_INF).astype(f32)


def _causal_bias():
    qi = lax.broadcasted_iota(jnp.int32, (WINDOW, WINDOW), 0)
    kj = lax.broadcasted_iota(jnp.int32, (WINDOW, WINDOW), 1)
    return jnp.where(qi >= kj, 0.0, NEG_INF).astype(f32)


def _nt_dot(a, b):
    return lax.dot_general(a, b, (((1,), (1,)), ((), ())), preferred_element_type=f32)


def _lane_lo():
    return lax.broadcasted_iota(jnp.int32, (1, LANES), 1) < HEAD_DIM


def _per_head(x, fill):
    lo = _lane_lo()
    f = jnp.full_like(x, fill)
    return jnp.where(lo, x, f), jnp.where(lo, f, x)


def _join_pair(a0, a1):
    lo = _lane_lo()
    return jnp.where(lo, a0, a1), pltpu.roll(jnp.where(lo, a1, a0), HEAD_DIM, 1)


def _run_pipeline(nblk, qk, softmax, pv, steps_per_trip):
    assert steps_per_trip % 2 == 0 and nblk % steps_per_trip == 0 and nblk >= 2 * steps_per_trip

    def step(t, slot):
        pv(t - 2, slot)
        qk(t, slot)
        softmax(t - 1, 1 - slot)

    qk(0, 0)
    qk(1, 1)
    softmax(0, 0)
    for t in range(2, steps_per_trip):
        step(t, t % 2)

    def trip(i, _):
        for u in range(steps_per_trip):
            step(steps_per_trip * i + u, u % 2)
        return 0

    lax.fori_loop(1, nblk // steps_per_trip, trip, 0)
    pv(nblk - 2, 0)
    softmax(nblk - 1, 1)
    pv(nblk - 1, 1)


def _block_rows(j):
    if isinstance(j, int):
        return pl.ds(max(j - 1, 0) * WINDOW, WINDOW), pl.ds(j * WINDOW, WINDOW)
    cur = pl.multiple_of(j * WINDOW, WINDOW)
    prev = pl.multiple_of(jnp.maximum(j - 1, 0) * WINDOW, WINDOW)
    return pl.ds(prev, WINDOW), pl.ds(cur, WINDOW)


def _swa_kernel(sink_ref, q_ref, k_ref, v_ref, o_ref, bias_ref, s_buf, p_buf, e_buf, kd_ref, vd_ref):
    seq = q_ref.shape[0]
    kvp = pl.program_id(1)
    lo = _lane_lo()

    @pl.when(jnp.logical_and(pl.program_id(0) == 0, kvp == 0))
    def _():
        bias_ref[0] = _band_bias(False)
        bias_ref[1] = _band_bias(True)

    lane_rolled_k = pltpu.roll(k_ref[...], HEAD_DIM, 1)
    lane_rolled_v = pltpu.roll(v_ref[...], HEAD_DIM, 1)
    for kvh in (0, 1):
        sel = lo if kvh == 0 else jnp.logical_not(lo)
        kd_ref[kvh] = jnp.where(sel, k_ref[...], lane_rolled_k)
        vd_ref[kvh] = jnp.where(sel, v_ref[...], lane_rolled_v)

    for kvh in (0, 1):
        pairs = (2 * kvh, 2 * kvh + 1)

        def kv_blocks(ref, j, fill, kvh=kvh):
            prev, cur = _block_rows(j)
            return _per_head(jnp.concatenate([ref[kvh, prev, :], ref[kvh, cur, :]], axis=0), fill)

        def qk(j, slot, pairs=pairs, kv_blocks=kv_blocks):
            _, cur = _block_rows(j)
            bias = bias_ref[jnp.where(j == 0, 1, 0)]
            ks = kv_blocks(kd_ref, j, 0)
            for pair in pairs:
                q2 = q_ref[cur, pair * LANES:(pair + 1) * LANES]
                for h in (0, 1):
                    s_buf[slot, 2 * pair + h] = _nt_dot(q2, ks[h]) + bias

        def softmax(j, slot, pairs=pairs):
            for pair in pairs:
                sink_terms = []
                for h in (0, 1):
                    sink = sink_ref[kvp * 2 * SWA_PAIRS + 2 * pair + h] * LOG2E
                    s = s_buf[slot, 2 * pair + h]
                    m = jnp.maximum(jnp.max(s, axis=-1, keepdims=True), sink)
                    p_buf[slot, 2 * pair + h] = jnp.exp2(s - m).astype(bf16)
                    sink_terms.append(jnp.exp2(sink - m))
                e_buf[slot, pair] = jnp.where(lo, sink_terms[0], sink_terms[1])

        def pv(j, slot, pairs=pairs, kv_blocks=kv_blocks):
            _, cur = _block_rows(j)
            vs = kv_blocks(vd_ref, j, 1)
            for pair in pairs:
                accs = [jnp.dot(p_buf[slot, 2 * pair + h], vs[h], preferred_element_type=f32) for h in (0, 1)]
                num, den = _join_pair(*accs)
                den = den + e_buf[slot, pair]
                o_ref[cur, pair * LANES:(pair + 1) * LANES] = (num / den).astype(bf16)

        _run_pipeline(seq // WINDOW, qk, softmax, pv, SWA_STEPS)


def _swa(proj, sinks, batch, seq):
    m = batch * seq
    qw = SWA_PAIRS * LANES
    nh = 2 * SWA_PAIRS
    return pl.pallas_call(
        _swa_kernel,
        grid=(batch, 2),
        in_specs=[pl.BlockSpec(memory_space=pltpu.SMEM),
                  pl.BlockSpec((seq, qw), lambda b, p: (b, COL_QB // qw + p)),
                  pl.BlockSpec((seq, LANES), lambda b, p: (b, COL_KB // LANES + p)),
                  pl.BlockSpec((seq, LANES), lambda b, p: (b, COL_VB // LANES + p))],
        out_specs=pl.BlockSpec((seq, qw), lambda b, p: (b, p)),
        out_shape=jax.ShapeDtypeStruct((m, D_MODEL), bf16),
        scratch_shapes=[pltpu.VMEM((2, WINDOW, 2 * WINDOW), f32),
                        pltpu.VMEM((2, nh, WINDOW, 2 * WINDOW), f32),
                        pltpu.VMEM((2, nh, WINDOW, 2 * WINDOW), bf16),
                        pltpu.VMEM((2, SWA_PAIRS, WINDOW, LANES), f32),
                        pltpu.VMEM((2, seq, LANES), bf16), pltpu.VMEM((2, seq, LANES), bf16)],
        compiler_params=_cparams(2),
        name="swa",
    )(sinks, proj, proj, proj)


def _dil_kernel(q_ref, k_ref, v_ref, o_ref, qf, kf, vf, q4, k4, v4, q4b, k4b, v4b,
                num1, den1, m1, num2, den2, m2, num3, den3, m3, out_f, bias_ref, causal_ref, s_buf, p_buf):
    seq = q_ref.shape[0]
    lo = _lane_lo()
    sub = seq // DIL_2
    nb2 = sub // WINDOW

    @pl.when(jnp.logical_and(pl.program_id(0) == 0, pl.program_id(1) == 0))
    def _():
        bias_ref[0] = _band_bias(False)
        bias_ref[1] = _band_bias(True)
        causal_ref[...] = _causal_bias()

    for src, nat, four, four_b in ((q_ref, qf, q4, q4b), (k_ref, kf, k4, k4b), (v_ref, vf, v4, v4b)):
        nat[...] = src[...].astype(f32)
        for r in range(DIL_2):
            part = nat[pl.ds(r, sub, stride=DIL_2), :]
            four[r * sub:(r + 1) * sub, :] = part
            four_b[r * sub:(r + 1) * sub, :] = part.astype(bf16)

    def rows2(j):
        r, jb = j // nb2, j % nb2
        cur = pl.multiple_of(r * sub + jb * WINDOW, WINDOW)
        prev = pl.multiple_of(r * sub + jnp.maximum(jb - 1, 0) * WINDOW, WINDOW)
        return pl.ds(prev, WINDOW), pl.ds(cur, WINDOW), jb == 0

    def rows3(j):
        return pl.ds((j % DIL_2) * sub + j // DIL_2, WINDOW, stride=DIL_2)

    def gather_kv(ref, ref4, ref4b, j):
        prev1, cur1 = _block_rows(j)
        prev2, cur2, _ = rows2(j)
        return (jnp.concatenate([ref[prev1, :], ref[cur1, :]], axis=0),
                jnp.concatenate([ref4b[prev2, :], ref4b[cur2, :]], axis=0),
                ref4[rows3(j), :].astype(bf16))

    def qk(j, slot):
        _, cur1 = _block_rows(j)
        _, cur2, first2 = rows2(j)
        qs = (q_ref[cur1, :], q4b[cur2, :], q4[rows3(j), :].astype(bf16))
        ks = gather_kv(k_ref, k4, k4b, j)
        biases = (bias_ref[jnp.where(j == 0, 1, 0)], bias_ref[jnp.where(first2, 1, 0)], causal_ref[...])
        for c in range(DIL_PATTERNS):
            for h, k_h in enumerate(_per_head(ks[c], 0)):
                s = _nt_dot(qs[c], k_h) + biases[c]
                s_buf[slot, 2 * c + h, :, 0:s.shape[1]] = s

    def softmax(j, slot):
        _, cur1 = _block_rows(j)
        _, cur2, _ = rows2(j)
        for c, (m_ref, rows) in enumerate(((m1, cur1), (m2, cur2), (m3, rows3(j)))):
            nk = WINDOW if c == 2 else 2 * WINDOW
            ms = []
            for h in (0, 1):
                s = s_buf[slot, 2 * c + h, :, 0:nk]
                m = jnp.max(s, axis=-1, keepdims=True)
                p_buf[slot, 2 * c + h, :, 0:nk] = jnp.exp2(s - m).astype(bf16)
                ms.append(m)
            m_ref[rows, :] = jnp.where(lo, ms[0], ms[1])

    def pv(j, slot):
        _, cur1 = _block_rows(j)
        _, cur2, _ = rows2(j)
        vs = gather_kv(v_ref, v4, v4b, j)
        dests = ((num1, den1, cur1), (num2, den2, cur2), (num3, den3, rows3(j)))
        for c, (num_ref, den_ref, rows) in enumerate(dests):
            nk = WINDOW if c == 2 else 2 * WINDOW
            accs = [jnp.dot(p_buf[slot, 2 * c + h, :, 0:nk], v_h, preferred_element_type=f32)
                    for h, v_h in enumerate(_per_head(vs[c], 1))]
            num, den = _join_pair(*accs)
            num_ref[rows, :] = num
            den_ref[rows, :] = den

    _run_pipeline(seq // WINDOW, qk, softmax, pv, DIL_STEPS)

    ct = 256
    per_res = sub // ct

    def combine(c, _):
        r, part = c // per_res, c % per_res
        rows4 = pl.ds(pl.multiple_of(c * ct, ct), ct)
        rows_nat = pl.ds(r + DIL_2 * ct * part, ct, stride=DIL_2)
        ma, mb, mc = m1[rows_nat, :], m2[rows4, :], m3[rows4, :]
        mx = jnp.maximum(jnp.maximum(ma, mb), mc)
        wa, wb, wc = jnp.exp2(ma - mx), jnp.exp2(mb - mx), jnp.exp2(mc - mx)
        num = wa * num1[rows_nat, :] + wb * num2[rows4, :] + wc * num3[rows4, :]
        den = wa * den1[rows_nat, :] + wb * den2[rows4, :] + wc * den3[rows4, :]
        out_f[rows_nat, :] = num / den
        return 0

    lax.fori_loop(0, seq // ct, combine, 0)
    o_ref[...] = out_f[...].astype(bf16)


def _dil(proj, batch, seq):
    m = batch * seq
    npair = D_MODEL // LANES
    assert seq == DIL_3 * WINDOW, "pattern 3 is written for one 128-token block per subsequence"
    spec = lambda col: pl.BlockSpec((seq, LANES), lambda b, p: (b, col // LANES + p))
    nh = 2 * DIL_PATTERNS
    tok_f = pltpu.VMEM((seq, LANES), f32)
    tok_b = pltpu.VMEM((seq, LANES), bf16)
    return pl.pallas_call(
        _dil_kernel,
        grid=(batch, npair),
        in_specs=[spec(COL_QC), spec(COL_KC), spec(COL_VC)],
        out_specs=pl.BlockSpec((seq, LANES), lambda b, p: (b, p)),
        out_shape=jax.ShapeDtypeStruct((m, D_MODEL), bf16),
        scratch_shapes=[tok_f] * 6 + [tok_b] * 3 + [tok_f] * 10
        + [pltpu.VMEM((2, WINDOW, 2 * WINDOW), f32), pltpu.VMEM((WINDOW, WINDOW), f32),
           pltpu.VMEM((2, nh, WINDOW, 2 * WINDOW), f32), pltpu.VMEM((2, nh, WINDOW, 2 * WINDOW), bf16)],
        compiler_params=_cparams(2),
        name="dil",
    )(proj, proj, proj)


def _layer_norm(z, g, b):
    mu = jnp.mean(z, axis=-1, keepdims=True)
    zc = z - mu
    var = jnp.mean(zc * zc, axis=-1, keepdims=True)
    return zc * lax.rsqrt(var + LN_EPS) * g + b


def _merge_kernel(x_ref, ya_ref, yb_ref, yc_ref, gates_ref, wb_ref, wo_ref, g_ref, b_ref, o_ref):
    for t in range(ROW_TM // ROW_SUB):
        rows = slice(t * ROW_SUB, (t + 1) * ROW_SUB)
        merged = None
        for n, y_ref in enumerate((ya_ref, yb_ref, yc_ref)):
            branch = jnp.dot(y_ref[rows, :], wb_ref[n], preferred_element_type=f32)
            gate = jax.nn.sigmoid(gates_ref[rows, n * D_MODEL:(n + 1) * D_MODEL].astype(f32))
            term = gate * branch
            merged = term if merged is None else merged + term
        mix = jnp.dot(merged.astype(bf16), wo_ref[...], preferred_element_type=f32)
        o_ref[rows, :] = _layer_norm(ALPHA * x_ref[rows, :] + mix, g_ref[...], b_ref[...])


def _merge(x, ya, yb, yc, proj, wb, wo, g, b, layer):
    m = x.shape[0]
    row = lambda width, col_block: pl.BlockSpec((ROW_TM, width), lambda i: (i, col_block))
    gw = N_BRANCHES * D_MODEL
    return pl.pallas_call(
        _merge_kernel,
        grid=(m // ROW_TM,),
        in_specs=[row(D_MODEL, 0), row(D_MODEL, 0), row(D_MODEL, 0), row(D_MODEL, 0), row(gw, COL_GATES // gw),
                  _resident((None, N_BRANCHES, D_MODEL, D_MODEL), lambda i: (layer, 0, 0, 0)),
                  _resident((None, D_MODEL, D_MODEL), lambda i: (layer, 0, 0)),
                  _resident((1, D_MODEL), lambda i: (0, 0)),
                  _resident((1, D_MODEL), lambda i: (0, 0))],
        out_specs=row(D_MODEL, 0),
        out_shape=jax.ShapeDtypeStruct((m, D_MODEL), f32),
        compiler_params=_cparams(1),
        name="merge",
    )(x, ya, yb, yc, proj, wb, wo, g, b)


def _ffn_kernel(x_ref, win_ref, wout_ref, g_ref, b_ref, o_ref):
    for t in range(ROW_TM // ROW_SUB):
        rows = slice(t * ROW_SUB, (t + 1) * ROW_SUB)
        x = x_ref[rows, :]
        xb = x.astype(bf16)
        acc = None
        for c in range(FF_HIDDEN // FFN_CH):
            h1 = jnp.dot(xb, win_ref[:, c * FFN_CH:(c + 1) * FFN_CH], preferred_element_type=f32)
            h3 = jnp.dot(xb, win_ref[:, FF_HIDDEN + c * FFN_CH:FF_HIDDEN + (c + 1) * FFN_CH],
                         preferred_element_type=f32)
            act = (jax.nn.silu(h1) * h3).astype(bf16)
            part = jnp.dot(act, wout_ref[c * FFN_CH:(c + 1) * FFN_CH, :], preferred_element_type=f32)
            acc = part if acc is None else acc + part
        o_ref[rows, :] = _layer_norm(ALPHA * x + acc, g_ref[...], b_ref[...])


def _ffn(x, win, wout, g, b, layer):
    m = x.shape[0]
    return pl.pallas_call(
        _ffn_kernel,
        grid=(m // ROW_TM,),
        in_specs=[pl.BlockSpec((ROW_TM, D_MODEL), lambda i: (i, 0)),
                  _resident((None, D_MODEL, 2 * FF_HIDDEN), lambda i: (layer, 0, 0)),
                  _resident((None, FF_HIDDEN, D_MODEL), lambda i: (layer, 0, 0)),
                  _resident((1, D_MODEL), lambda i: (0, 0)),
                  _resident((1, D_MODEL), lambda i: (0, 0))],
        out_specs=pl.BlockSpec((ROW_TM, D_MODEL), lambda i: (i, 0)),
        out_shape=jax.ShapeDtypeStruct((m, D_MODEL), f32),
        compiler_params=_cparams(1),
        name="ffn",
    )(x, win, wout, g, b)


def _prep_w_in(w_in):
    scale = HEAD_DIM ** -0.5 * LOG2E
    lru_x, lru_g, qb, kb, vb, qc, kc, vc, gates = jnp.split(
        w_in, [1024, 2048, 3072, 3328, 3584, 4608, 5632, 6656], axis=-1)
    pieces = [lru_x, lru_g, gates, qb * scale, qc * scale, kc, vc, kb, vb]
    return jnp.concatenate([p.astype(bf16) for p in pieces], axis=-1)


def _block_diag(w):
    per = LRU_CW // HEAD_DIM
    w = w.reshape(DEPTH, LRU_BLOCKS // per, per, HEAD_DIM, HEAD_DIM)
    eye = jnp.eye(per, dtype=w.dtype)
    bd = jnp.einsum("lgaij,ab->lgaibj", w, eye)
    return bd.reshape(DEPTH, LRU_BLOCKS // per, LRU_CW, LRU_CW).astype(bf16)


def kernel(x, w_in, conv_w, conv_b, w_rg, b_rg, w_ig, b_ig, lru_lambda, sinks, w_branch, w_out,
           ln1_g, ln1_b, w_ffn_in, w_ffn_out, ln2_g, ln2_b):
    batch, seq, d = x.shape
    assert d == D_MODEL and w_in.shape == (DEPTH, D_MODEL, IN_WIDTH)
    w_in_b = _prep_w_in(w_in)
    wr_b = _block_diag(w_rg)
    wi_b = _block_diag(w_ig)
    wb_b = w_branch.astype(bf16)
    wo_b = w_out.astype(bf16)
    wfi_b = w_ffn_in.astype(bf16)
    wfo_b = w_ffn_out.astype(bf16)
    row = lambda p, l: p[l].reshape(1, D_MODEL)

    h = x.reshape(batch * seq, D_MODEL)
    for l in range(DEPTH):
        proj, ya = _proj_lru(h, w_in_b, conv_w[l], row(conv_b, l), wr_b, row(b_rg, l), wi_b,
                             row(b_ig, l), row(lru_lambda, l), seq, l)
        yb = _swa(proj, sinks[l], batch, seq)
        yc = _dil(proj, batch, seq)
        h = _merge(h, ya, yb, yc, proj, wb_b, wo_b, row(ln1_g, l), row(ln1_b, l), l)
        h = _ffn(h, wfi_b, wfo_b, row(ln2_g, l), row(ln2_b, l), l)
    return h.reshape(batch, seq, D_MODEL)
```

```python
import functools

import jax
import jax.numpy as jnp
from jax import lax
from jax.experimental import pallas as pl
from jax.experimental.pallas import tpu as pltpu

f32 = jnp.float32
bf16 = jnp.bfloat16

D_MODEL = 1024
DEPTH = 4
HEAD_DIM = 64
N_BRANCHES = 3
LRU_BLOCKS = 16
CONV_WIDTH = 4
LRU_C = 8.0
WINDOW = 128
DIL_2, DIL_3 = 4, 16
FF_HIDDEN = 2816
ALPHA = (2.0 * DEPTH) ** 0.25
LN_EPS = 1e-5
NEG_INF = -1e30
LOG2E = 1.4426950408889634

LANES = 128
SUBLANES = 8
VMEM_LIMIT = 56 * 1024 * 1024

IN_WIDTH = 9728
LRU_COLS = 2 * D_MODEL
REST_WIDTH = IN_WIDTH - LRU_COLS
COL_GATES = 0
COL_QB = 3072
COL_QC = 4096
COL_KC = 5120
COL_VC = 6144
COL_KB = 7168
COL_VB = 7424

PROJ_TM = 512
PROJ_TN = 512
LRU_CW = 256
LRU_T = 128
ROW_TM = 1024
ROW_SUB = 512
FFN_CH = 256
SWA_PAIRS = 4
DIL_PATTERNS = 3
SWA_STEPS = 4
DIL_STEPS = 8


def _cparams(n_grid):
    return pltpu.CompilerParams(dimension_semantics=("arbitrary",) * n_grid, vmem_limit_bytes=VMEM_LIMIT)


def _resident(block_shape, index_map):
    return pl.BlockSpec(block_shape, index_map, pipeline_mode=pl.Buffered(1))


def _conv(xe, cw, cb):
    xc = cb + cw[CONV_WIDTH - 1:CONV_WIDTH] * xe[SUBLANES:]
    for s in range(1, CONV_WIDTH):
        xc = xc + cw[CONV_WIDTH - 1 - s:CONV_WIDTH - s] * pltpu.roll(xe, s, 0)[SUBLANES:]
    return xc


SCAN_ROWS = 4


def _lru_rows(xc, r_pre, i_pre, gate, carry, br, bi, sp, scan_ref):
    rows, width = xc.shape
    r = jax.nn.sigmoid(r_pre + br)
    i = jax.nn.sigmoid(i_pre + bi)
    log_a = (-LRU_C * r) * sp
    a = jnp.exp(log_a)
    mult = jnp.sqrt(1.0 - a * a)
    b = mult * (i * xc)
    a_ref, b_ref, h_ref = scan_ref.at[0], scan_ref.at[1], scan_ref.at[2]
    block = SCAN_ROWS * SUBLANES
    row = lax.broadcasted_iota(jnp.int32, (SUBLANES, LANES), 0)
    carries = []
    for lc in range(width // LANES):
        lanes = slice(lc * LANES, (lc + 1) * LANES)
        a_ref[lc] = a[:, lanes]
        b_ref[lc] = b[:, lanes]
        c_in = carry[:, lanes]
        for blk in range(rows // block):
            run = lambda ref, j: ref[lc, pl.ds(blk * block + j, SUBLANES, stride=SCAN_ROWS), :]
            hs, ps = [run(b_ref, 0)], [run(a_ref, 0)]
            for j in range(1, SCAN_ROWS):
                a_j = run(a_ref, j)
                hs.append(a_j * hs[-1] + run(b_ref, j))
                ps.append(a_j * ps[-1])
            q, e = ps[-1], hs[-1]
            for d in (1, 2, 4):
                q_s = jnp.where(row >= d, pltpu.roll(q, d, 0), 1.0)
                e_s = jnp.where(row >= d, pltpu.roll(e, d, 0), 0.0)
                e = q * e_s + e
                q = q * q_s
            after = q * c_in + e
            before = jnp.where(row == 0, c_in, pltpu.roll(after, 1, 0))
            c_in = after[SUBLANES - 1:SUBLANES, :]
            for j in range(SCAN_ROWS):
                h_ref[lc, pl.ds(blk * block + j, SUBLANES, stride=SCAN_ROWS), :] = hs[j] + ps[j] * before
        carries.append(c_in)
    h = jnp.concatenate([h_ref[lc] for lc in range(width // LANES)], axis=1)
    return (h * jax.nn.gelu(gate)).astype(bf16), jnp.concatenate(carries, axis=1)


def _proj_lru_kernel(tiles_per_seq, x_ref, w_ref, cw_ref, cb_ref, wr_ref, br_ref, wi_ref, bi_ref, lam_ref,
                     rest_ref, ya_ref, lx_ref, gt_ref, carry_ref, scan_ref):
    i = pl.program_id(0)
    slot = i % 2
    rslot = 1 - slot
    n_rc = PROJ_TM // LRU_T
    n_cg = D_MODEL // LRU_CW
    n_rest = REST_WIDTH // PROJ_TN
    assert n_rc * n_cg == n_rest + 1

    @pl.when(i == 0)
    def _():
        lx_ref[...] = jnp.zeros(lx_ref.shape, f32)
        gt_ref[...] = jnp.zeros(gt_ref.shape, f32)
        carry_ref[...] = jnp.zeros(carry_ref.shape, f32)

    prev_starts_seq = (i - 1) % tiles_per_seq == 0
    this_starts_seq = i % tiles_per_seq == 0
    xb = x_ref[...].astype(bf16)

    def lru_group(cg):
        lanes = slice(cg * LRU_CW, (cg + 1) * LRU_CW)
        xc = _conv(lx_ref[rslot, :, lanes], cw_ref[:, lanes], cb_ref[:, lanes])
        xcb = xc.astype(bf16)
        r_pre = jnp.dot(xcb, wr_ref[cg], preferred_element_type=f32)
        i_pre = jnp.dot(xcb, wi_ref[cg], preferred_element_type=f32)
        sp = jax.nn.softplus(-lam_ref[:, lanes])
        state = {"carry": jnp.where(prev_starts_seq, 0.0, carry_ref[:, lanes])}

        def finish(rc):
            rows = slice(rc * LRU_T, (rc + 1) * LRU_T)
            y, state["carry"] = _lru_rows(xc[rows], r_pre[rows], i_pre[rows], gt_ref[rslot, rows, lanes],
                                          state["carry"], br_ref[:, lanes], bi_ref[:, lanes], sp,
                                          scan_ref.at[rc % 2])
            ya_ref[rows, lanes] = y
            if rc == n_rc - 1:
                carry_ref[:, lanes] = state["carry"]

        return finish

    finish = lru_group(0)
    lx_ref[slot, SUBLANES:, :] = jnp.dot(xb, w_ref[:, 0:D_MODEL], preferred_element_type=f32)
    gt_ref[slot] = jnp.dot(xb, w_ref[:, D_MODEL:LRU_COLS], preferred_element_type=f32)
    tail = lx_ref[rslot, PROJ_TM:PROJ_TM + SUBLANES, :]
    lx_ref[slot, 0:SUBLANES, :] = jnp.where(this_starts_seq, 0.0, tail)

    for c in range(n_rc * n_cg):
        cg, rc = c // n_rc, c % n_rc
        if rc == 0 and cg > 0:
            finish = lru_group(cg)
        if c < n_rest:
            cols = slice(c * PROJ_TN, (c + 1) * PROJ_TN)
            acc = jnp.dot(xb, w_ref[:, LRU_COLS + c * PROJ_TN:LRU_COLS + (c + 1) * PROJ_TN],
                          preferred_element_type=f32)
            rest_ref[:, cols] = acc.astype(bf16)
        finish(rc)


def _proj_lru(x, w, cw, cb, wr, br, wi, bi, lam, seq, layer):
    m = x.shape[0]
    nt = m // PROJ_TM
    last = nt - 1
    vec = lambda rows: _resident((rows, D_MODEL), lambda i: (0, 0))
    mat = _resident((None, D_MODEL // LRU_CW, LRU_CW, LRU_CW), lambda i: (layer, 0, 0, 0))
    return pl.pallas_call(
        functools.partial(_proj_lru_kernel, seq // PROJ_TM),
        grid=(nt + 1,),
        in_specs=[pl.BlockSpec((PROJ_TM, D_MODEL), lambda i: (jnp.minimum(i, last), 0)),
                  _resident((None, D_MODEL, IN_WIDTH), lambda i: (layer, 0, 0)),
                  vec(CONV_WIDTH), vec(1), mat, vec(1), mat, vec(1), vec(1)],
        out_specs=[pl.BlockSpec((PROJ_TM, REST_WIDTH), lambda i: (jnp.minimum(i, last), 0)),
                   pl.BlockSpec((PROJ_TM, D_MODEL), lambda i: (jnp.maximum(i - 1, 0), 0))],
        out_shape=[jax.ShapeDtypeStruct((m, REST_WIDTH), bf16), jax.ShapeDtypeStruct((m, D_MODEL), bf16)],
        scratch_shapes=[pltpu.VMEM((2, PROJ_TM + SUBLANES, D_MODEL), f32),
                        pltpu.VMEM((2, PROJ_TM, D_MODEL), f32),
                        pltpu.VMEM((1, D_MODEL), f32),
                        pltpu.VMEM((2, 3, LRU_CW // LANES, LRU_T, LANES), f32)],
        compiler_params=_cparams(1),
        name="proj_lru",
    )(x, w, cw, cb, wr, br, wi, bi, lam)


def _band_bias(first_block):
    qi = lax.broadcasted_iota(jnp.int32, (WINDOW, 2 * WINDOW), 0)
    kj = lax.broadcasted_iota(jnp.int32, (WINDOW, 2 * WINDOW), 1)
    rel = qi + WINDOW - kj
    ok = jnp.logical_and(rel >= 0, rel <= WINDOW)
    if first_block:
        ok = jnp.logical_and(ok, kj >= WINDOW)
    return jnp.where(ok, 0.0, NEG_INF).astype(f32)


def _causal_bias():
    qi = lax.broadcasted_iota(jnp.int32, (WINDOW, WINDOW), 0)
    kj = lax.broadcasted_iota(jnp.int32, (WINDOW, WINDOW), 1)
    return jnp.where(qi >= kj, 0.0, NEG_INF).astype(f32)


def _nt_dot(a, b):
    return lax.dot_general(a, b, (((1,), (1,)), ((), ())), preferred_element_type=f32)


def _lane_lo():
    return lax.broadcasted_iota(jnp.int32, (1, LANES), 1) < HEAD_DIM


def _per_head(x, fill):
    lo = _lane_lo()
    f = jnp.full_like(x, fill)
    return jnp.where(lo, x, f), jnp.where(lo, f, x)


def _join_pair(a0, a1):
    lo = _lane_lo()
    return jnp.where(lo, a0, a1), pltpu.roll(jnp.where(lo, a1, a0), HEAD_DIM, 1)


def _run_pipeline(nblk, qk, softmax, pv, steps_per_trip):
    assert steps_per_trip % 2 == 0 and nblk % steps_per_trip == 0 and nblk >= 2 * steps_per_trip

    def step(t, slot):
        pv(t - 2, slot)
        qk(t, slot)
        softmax(t - 1, 1 - slot)

    qk(0, 0)
    qk(1, 1)
    softmax(0, 0)
    for t in range(2, steps_per_trip):
        step(t, t % 2)

    def trip(i, _):
        for u in range(steps_per_trip):
            step(steps_per_trip * i + u, u % 2)
        return 0

    lax.fori_loop(1, nblk // steps_per_trip, trip, 0)
    pv(nblk - 2, 0)
    softmax(nblk - 1, 1)
    pv(nblk - 1, 1)


def _block_rows(j):
    if isinstance(j, int):
        return pl.ds(max(j - 1, 0) * WINDOW, WINDOW), pl.ds(j * WINDOW, WINDOW)
    cur = pl.multiple_of(j * WINDOW, WINDOW)
    prev = pl.multiple_of(jnp.maximum(j - 1, 0) * WINDOW, WINDOW)
    return pl.ds(prev, WINDOW), pl.ds(cur, WINDOW)


def _swa_kernel(sink_ref, q_ref, k_ref, v_ref, o_ref, bias_ref, s_buf, p_buf, e_buf, kd_ref, vd_ref):
    seq = q_ref.shape[0]
    kvp = pl.program_id(1)
    lo = _lane_lo()

    @pl.when(jnp.logical_and(pl.program_id(0) == 0, kvp == 0))
    def _():
        bias_ref[0] = _band_bias(False)
        bias_ref[1] = _band_bias(True)

    lane_rolled_k = pltpu.roll(k_ref[...], HEAD_DIM, 1)
    lane_rolled_v = pltpu.roll(v_ref[...], HEAD_DIM, 1)
    for kvh in (0, 1):
        sel = lo if kvh == 0 else jnp.logical_not(lo)
        kd_ref[kvh] = jnp.where(sel, k_ref[...], lane_rolled_k)
        vd_ref[kvh] = jnp.where(sel, v_ref[...], lane_rolled_v)

    for kvh in (0, 1):
        pairs = (2 * kvh, 2 * kvh + 1)

        def kv_blocks(ref, j, fill, kvh=kvh):
            prev, cur = _block_rows(j)
            return _per_head(jnp.concatenate([ref[kvh, prev, :], ref[kvh, cur, :]], axis=0), fill)

        def qk(j, slot, pairs=pairs, kv_blocks=kv_blocks):
            _, cur = _block_rows(j)
            bias = bias_ref[jnp.where(j == 0, 1, 0)]
            ks = kv_blocks(kd_ref, j, 0)
            for pair in pairs:
                q2 = q_ref[cur, pair * LANES:(pair + 1) * LANES]
                for h in (0, 1):
                    s_buf[slot, 2 * pair + h] = _nt_dot(q2, ks[h]) + bias

        def softmax(j, slot, pairs=pairs):
            for pair in pairs:
                sink_terms = []
                for h in (0, 1):
                    sink = sink_ref[kvp * 2 * SWA_PAIRS + 2 * pair + h] * LOG2E
                    s = s_buf[slot, 2 * pair + h]
                    m = jnp.maximum(jnp.max(s, axis=-1, keepdims=True), sink)
                    p_buf[slot, 2 * pair + h] = jnp.exp2(s - m).astype(bf16)
                    sink_terms.append(jnp.exp2(sink - m))
                e_buf[slot, pair] = jnp.where(lo, sink_terms[0], sink_terms[1])

        def pv(j, slot, pairs=pairs, kv_blocks=kv_blocks):
            _, cur = _block_rows(j)
            vs = kv_blocks(vd_ref, j, 1)
            for pair in pairs:
                accs = [jnp.dot(p_buf[slot, 2 * pair + h], vs[h], preferred_element_type=f32) for h in (0, 1)]
                num, den = _join_pair(*accs)
                den = den + e_buf[slot, pair]
                o_ref[cur, pair * LANES:(pair + 1) * LANES] = (num / den).astype(bf16)

        _run_pipeline(seq // WINDOW, qk, softmax, pv, SWA_STEPS)


def _swa(proj, sinks, batch, seq):
    m = batch * seq
    qw = SWA_PAIRS * LANES
    nh = 2 * SWA_PAIRS
    return pl.pallas_call(
        _swa_kernel,
        grid=(batch, 2),
        in_specs=[pl.BlockSpec(memory_space=pltpu.SMEM),
                  pl.BlockSpec((seq, qw), lambda b, p: (b, COL_QB // qw + p)),
                  pl.BlockSpec((seq, LANES), lambda b, p: (b, COL_KB // LANES + p)),
                  pl.BlockSpec((seq, LANES), lambda b, p: (b, COL_VB // LANES + p))],
        out_specs=pl.BlockSpec((seq, qw), lambda b, p: (b, p)),
        out_shape=jax.ShapeDtypeStruct((m, D_MODEL), bf16),
        scratch_shapes=[pltpu.VMEM((2, WINDOW, 2 * WINDOW), f32),
                        pltpu.VMEM((2, nh, WINDOW, 2 * WINDOW), f32),
                        pltpu.VMEM((2, nh, WINDOW, 2 * WINDOW), bf16),
                        pltpu.VMEM((2, SWA_PAIRS, WINDOW, LANES), f32),
                        pltpu.VMEM((2, seq, LANES), bf16), pltpu.VMEM((2, seq, LANES), bf16)],
        compiler_params=_cparams(2),
        name="swa",
    )(sinks, proj, proj, proj)


def _dil_kernel(q_ref, k_ref, v_ref, o_ref, qf, kf, vf, q4, k4, v4, q4b, k4b, v4b,
                num1, den1, m1, num2, den2, m2, num3, den3, m3, out_f, bias_ref, causal_ref, s_buf, p_buf):
    seq = q_ref.shape[0]
    lo = _lane_lo()
    sub = seq // DIL_2
    nb2 = sub // WINDOW

    @pl.when(jnp.logical_and(pl.program_id(0) == 0, pl.program_id(1) == 0))
    def _():
        bias_ref[0] = _band_bias(False)
        bias_ref[1] = _band_bias(True)
        causal_ref[...] = _causal_bias()

    for src, nat, four, four_b in ((q_ref, qf, q4, q4b), (k_ref, kf, k4, k4b), (v_ref, vf, v4, v4b)):
        nat[...] = src[...].astype(f32)
        for r in range(DIL_2):
            part = nat[pl.ds(r, sub, stride=DIL_2), :]
            four[r * sub:(r + 1) * sub, :] = part
            four_b[r * sub:(r + 1) * sub, :] = part.astype(bf16)

    def rows2(j):
        r, jb = j // nb2, j % nb2
        cur = pl.multiple_of(r * sub + jb * WINDOW, WINDOW)
        prev = pl.multiple_of(r * sub + jnp.maximum(jb - 1, 0) * WINDOW, WINDOW)
        return pl.ds(prev, WINDOW), pl.ds(cur, WINDOW), jb == 0

    def rows3(j):
        return pl.ds((j % DIL_2) * sub + j // DIL_2, WINDOW, stride=DIL_2)

    def gather_kv(ref, ref4, ref4b, j):
        prev1, cur1 = _block_rows(j)
        prev2, cur2, _ = rows2(j)
        return (jnp.concatenate([ref[prev1, :], ref[cur1, :]], axis=0),
                jnp.concatenate([ref4b[prev2, :], ref4b[cur2, :]], axis=0),
                ref4[rows3(j), :].astype(bf16))

    def qk(j, slot):
        _, cur1 = _block_rows(j)
        _, cur2, first2 = rows2(j)
        qs = (q_ref[cur1, :], q4b[cur2, :], q4[rows3(j), :].astype(bf16))
        ks = gather_kv(k_ref, k4, k4b, j)
        biases = (bias_ref[jnp.where(j == 0, 1, 0)], bias_ref[jnp.where(first2, 1, 0)], causal_ref[...])
        for c in range(DIL_PATTERNS):
            for h, k_h in enumerate(_per_head(ks[c], 0)):
                s = _nt_dot(qs[c], k_h) + biases[c]
                s_buf[slot, 2 * c + h, :, 0:s.shape[1]] = s

    def softmax(j, slot):
        _, cur1 = _block_rows(j)
        _, cur2, _ = rows2(j)
        for c, (m_ref, rows) in enumerate(((m1, cur1), (m2, cur2), (m3, rows3(j)))):
            nk = WINDOW if c == 2 else 2 * WINDOW
            ms = []
            for h in (0, 1):
                s = s_buf[slot, 2 * c + h, :, 0:nk]
                m = jnp.max(s, axis=-1, keepdims=True)
                p_buf[slot, 2 * c + h, :, 0:nk] = jnp.exp2(s - m).astype(bf16)
                ms.append(m)
            m_ref[rows, :] = jnp.where(lo, ms[0], ms[1])

    def pv(j, slot):
        _, cur1 = _block_rows(j)
        _, cur2, _ = rows2(j)
        vs = gather_kv(v_ref, v4, v4b, j)
        dests = ((num1, den1, cur1), (num2, den2, cur2), (num3, den3, rows3(j)))
        for c, (num_ref, den_ref, rows) in enumerate(dests):
            nk = WINDOW if c == 2 else 2 * WINDOW
            accs = [jnp.dot(p_buf[slot, 2 * c + h, :, 0:nk], v_h, preferred_element_type=f32)
                    for h, v_h in enumerate(_per_head(vs[c], 1))]
            num, den = _join_pair(*accs)
            num_ref[rows, :] = num
            den_ref[rows, :] = den

    _run_pipeline(seq // WINDOW, qk, softmax, pv, DIL_STEPS)

    ct = 256
    per_res = sub // ct

    def combine(c, _):
        r, part = c // per_res, c % per_res
        rows4 = pl.ds(pl.multiple_of(c * ct, ct), ct)
        rows_nat = pl.ds(r + DIL_2 * ct * part, ct, stride=DIL_2)
        ma, mb, mc = m1[rows_nat, :], m2[rows4, :], m3[rows4, :]
        mx = jnp.maximum(jnp.maximum(ma, mb), mc)
        wa, wb, wc = jnp.exp2(ma - mx), jnp.exp2(mb - mx), jnp.exp2(mc - mx)
        num = wa * num1[rows_nat, :] + wb * num2[rows4, :] + wc * num3[rows4, :]
        den = wa * den1[rows_nat, :] + wb * den2[rows4, :] + wc * den3[rows4, :]
        out_f[rows_nat, :] = num / den
        return 0

    lax.fori_loop(0, seq // ct, combine, 0)
    o_ref[...] = out_f[...].astype(bf16)


def _dil(proj, batch, seq):
    m = batch * seq
    npair = D_MODEL // LANES
    assert seq == DIL_3 * WINDOW, "pattern 3 is written for one 128-token block per subsequence"
    spec = lambda col: pl.BlockSpec((seq, LANES), lambda b, p: (b, col // LANES + p))
    nh = 2 * DIL_PATTERNS
    tok_f = pltpu.VMEM((seq, LANES), f32)
    tok_b = pltpu.VMEM((seq, LANES), bf16)
    return pl.pallas_call(
        _dil_kernel,
        grid=(batch, npair),
        in_specs=[spec(COL_QC), spec(COL_KC), spec(COL_VC)],
        out_specs=pl.BlockSpec((seq, LANES), lambda b, p: (b, p)),
        out_shape=jax.ShapeDtypeStruct((m, D_MODEL), bf16),
        scratch_shapes=[tok_f] * 6 + [tok_b] * 3 + [tok_f] * 10
        + [pltpu.VMEM((2, WINDOW, 2 * WINDOW), f32), pltpu.VMEM((WINDOW, WINDOW), f32),
           pltpu.VMEM((2, nh, WINDOW, 2 * WINDOW), f32), pltpu.VMEM((2, nh, WINDOW, 2 * WINDOW), bf16)],
        compiler_params=_cparams(2),
        name="dil",
    )(proj, proj, proj)


def _layer_norm(z, g, b):
    mu = jnp.mean(z, axis=-1, keepdims=True)
    zc = z - mu
    var = jnp.mean(zc * zc, axis=-1, keepdims=True)
    return zc * lax.rsqrt(var + LN_EPS) * g + b


def _merge_kernel(x_ref, ya_ref, yb_ref, yc_ref, gates_ref, wb_ref, wo_ref, g_ref, b_ref, o_ref):
    for t in range(ROW_TM // ROW_SUB):
        rows = slice(t * ROW_SUB, (t + 1) * ROW_SUB)
        merged = None
        for n, y_ref in enumerate((ya_ref, yb_ref, yc_ref)):
            branch = jnp.dot(y_ref[rows, :], wb_ref[n], preferred_element_type=f32)
            gate = jax.nn.sigmoid(gates_ref[rows, n * D_MODEL:(n + 1) * D_MODEL].astype(f32))
            term = gate * branch
            merged = term if merged is None else merged + term
        mix = jnp.dot(merged.astype(bf16), wo_ref[...], preferred_element_type=f32)
        o_ref[rows, :] = _layer_norm(ALPHA * x_ref[rows, :] + mix, g_ref[...], b_ref[...])


def _merge(x, ya, yb, yc, proj, wb, wo, g, b, layer):
    m = x.shape[0]
    row = lambda width, col_block: pl.BlockSpec((ROW_TM, width), lambda i: (i, col_block))
    gw = N_BRANCHES * D_MODEL
    return pl.pallas_call(
        _merge_kernel,
        grid=(m // ROW_TM,),
        in_specs=[row(D_MODEL, 0), row(D_MODEL, 0), row(D_MODEL, 0), row(D_MODEL, 0), row(gw, COL_GATES // gw),
                  _resident((None, N_BRANCHES, D_MODEL, D_MODEL), lambda i: (layer, 0, 0, 0)),
                  _resident((None, D_MODEL, D_MODEL), lambda i: (layer, 0, 0)),
                  _resident((1, D_MODEL), lambda i: (0, 0)),
                  _resident((1, D_MODEL), lambda i: (0, 0))],
        out_specs=row(D_MODEL, 0),
        out_shape=jax.ShapeDtypeStruct((m, D_MODEL), f32),
        compiler_params=_cparams(1),
        name="merge",
    )(x, ya, yb, yc, proj, wb, wo, g, b)


def _ffn_kernel(x_ref, win_ref, wout_ref, g_ref, b_ref, o_ref):
    for t in range(ROW_TM // ROW_SUB):
        rows = slice(t * ROW_SUB, (t + 1) * ROW_SUB)
        x = x_ref[rows, :]
        xb = x.astype(bf16)
        acc = None
        for c in range(FF_HIDDEN // FFN_CH):
            h1 = jnp.dot(xb, win_ref[:, c * FFN_CH:(c + 1) * FFN_CH], preferred_element_type=f32)
            h3 = jnp.dot(xb, win_ref[:, FF_HIDDEN + c * FFN_CH:FF_HIDDEN + (c + 1) * FFN_CH],
                         preferred_element_type=f32)
            act = (jax.nn.silu(h1) * h3).astype(bf16)
            part = jnp.dot(act, wout_ref[c * FFN_CH:(c + 1) * FFN_CH, :], preferred_element_type=f32)
            acc = part if acc is None else acc + part
        o_ref[rows, :] = _layer_norm(ALPHA * x + acc, g_ref[...], b_ref[...])


def _ffn(x, win, wout, g, b, layer):
    m = x.shape[0]
    return pl.pallas_call(
        _ffn_kernel,
        grid=(m // ROW_TM,),
        in_specs=[pl.BlockSpec((ROW_TM, D_MODEL), lambda i: (i, 0)),
                  _resident((None, D_MODEL, 2 * FF_HIDDEN), lambda i: (layer, 0, 0)),
                  _resident((None, FF_HIDDEN, D_MODEL), lambda i: (layer, 0, 0)),
                  _resident((1, D_MODEL), lambda i: (0, 0)),
                  _resident((1, D_MODEL), lambda i: (0, 0))],
        out_specs=pl.BlockSpec((ROW_TM, D_MODEL), lambda i: (i, 0)),
        out_shape=jax.ShapeDtypeStruct((m, D_MODEL), f32),
        compiler_params=_cparams(1),
        name="ffn",
    )(x, win, wout, g, b)


def _prep_w_in(w_in):
    scale = HEAD_DIM ** -0.5 * LOG2E
    lru_x, lru_g, qb, kb, vb, qc, kc, vc, gates = jnp.split(
        w_in, [1024, 2048, 3072, 3328, 3584, 4608, 5632, 6656], axis=-1)
    pieces = [lru_x, lru_g, gates, qb * scale, qc * scale, kc, vc, kb, vb]
    return jnp.concatenate([p.astype(bf16) for p in pieces], axis=-1)


def _block_diag(w):
    per = LRU_CW // HEAD_DIM
    w = w.reshape(DEPTH, LRU_BLOCKS // per, per, HEAD_DIM, HEAD_DIM)
    eye = jnp.eye(per, dtype=w.dtype)
    bd = jnp.einsum("lgaij,ab->lgaibj", w, eye)
    return bd.reshape(DEPTH, LRU_BLOCKS // per, LRU_CW, LRU_CW).astype(bf16)


def kernel(x, w_in, conv_w, conv_b, w_rg, b_rg, w_ig, b_ig, lru_lambda, sinks, w_branch, w_out,
           ln1_g, ln1_b, w_ffn_in, w_ffn_out, ln2_g, ln2_b):
    batch, seq, d = x.shape
    assert d == D_MODEL and w_in.shape == (DEPTH, D_MODEL, IN_WIDTH)
    w_in_b = _prep_w_in(w_in)
    wr_b = _block_diag(w_rg)
    wi_b = _block_diag(w_ig)
    wb_b = w_branch.astype(bf16)
    wo_b = w_out.astype(bf16)
    wfi_b = w_ffn_in.astype(bf16)
    wfo_b = w_ffn_out.astype(bf16)
    row = lambda p, l: p[l].reshape(1, D_MODEL)

    h = x.reshape(batch * seq, D_MODEL)
    for l in range(DEPTH):
        proj, ya = _proj_lru(h, w_in_b, conv_w[l], row(conv_b, l), wr_b, row(b_rg, l), wi_b,
                             row(b_ig, l), row(lru_lambda, l), seq, l)
        yb = _swa(proj, sinks[l], batch, seq)
        yc = _dil(proj, batch, seq)
        h = _merge(h, ya, yb, yc, proj, wb_b, wo_b, row(ln1_g, l), row(ln1_b, l), l)
        h = _ffn(h, wfi_b, wfo_b, row(ln2_g, l), row(ln2_b, l), l)
    return h.reshape(batch, seq, D_MODEL)
```

```python
import functools

import jax
import jax.numpy as jnp
from jax import lax
from jax.experimental import pallas as pl
from jax.experimental.pallas import tpu as pltpu

f32 = jnp.float32
bf16 = jnp.bfloat16

D_MODEL = 1024
DEPTH = 4
HEAD_DIM = 64
N_BRANCHES = 3
LRU_BLOCKS = 16
CONV_WIDTH = 4
LRU_C = 8.0
WINDOW = 128
DIL_2, DIL_3 = 4, 16
FF_HIDDEN = 2816
ALPHA = (2.0 * DEPTH) ** 0.25
LN_EPS = 1e-5
NEG_INF = -1e30
LOG2E = 1.4426950408889634

LANES = 128
SUBLANES = 8
VMEM_LIMIT = 56 * 1024 * 1024

IN_WIDTH = 9728
LRU_COLS = 2 * D_MODEL
REST_WIDTH = IN_WIDTH - LRU_COLS
COL_GATES = 0
COL_QB = 3072
COL_QC = 4096
COL_KC = 5120
COL_VC = 6144
COL_KB = 7168
COL_VB = 7424

PROJ_TM = 512
PROJ_TN = 512
LRU_CW = 256
LRU_T = 128
ROW_TM = 1024
ROW_SUB = 512
FFN_CH = 256
SWA_PAIRS = 4
DIL_PATTERNS = 3
SWA_STEPS = 2
DIL_STEPS = 8


def _cparams(n_grid):
    return pltpu.CompilerParams(dimension_semantics=("arbitrary",) * n_grid, vmem_limit_bytes=VMEM_LIMIT)


def _resident(block_shape, index_map):
    return pl.BlockSpec(block_shape, index_map, pipeline_mode=pl.Buffered(1))


def _conv(xe, cw, cb):
    xc = cb + cw[CONV_WIDTH - 1:CONV_WIDTH] * xe[SUBLANES:]
    for s in range(1, CONV_WIDTH):
        xc = xc + cw[CONV_WIDTH - 1 - s:CONV_WIDTH - s] * pltpu.roll(xe, s, 0)[SUBLANES:]
    return xc


SCAN_ROWS = 4


def _lru_rows(xc, r_pre, i_pre, gate, carry, br, bi, sp, scan_ref):
    rows, width = xc.shape
    r = jax.nn.sigmoid(r_pre + br)
    i = jax.nn.sigmoid(i_pre + bi)
    log_a = (-LRU_C * r) * sp
    a = jnp.exp(log_a)
    mult = jnp.sqrt(1.0 - a * a)
    b = mult * (i * xc)
    a_ref, b_ref, h_ref = scan_ref.at[0], scan_ref.at[1], scan_ref.at[2]
    block = SCAN_ROWS * SUBLANES
    row = lax.broadcasted_iota(jnp.int32, (SUBLANES, LANES), 0)
    carries = []
    for lc in range(width // LANES):
        lanes = slice(lc * LANES, (lc + 1) * LANES)
        a_ref[lc] = a[:, lanes]
        b_ref[lc] = b[:, lanes]
        c_in = carry[:, lanes]
        for blk in range(rows // block):
            run = lambda ref, j: ref[lc, pl.ds(blk * block + j, SUBLANES, stride=SCAN_ROWS), :]
            hs, ps = [run(b_ref, 0)], [run(a_ref, 0)]
            for j in range(1, SCAN_ROWS):
                a_j = run(a_ref, j)
                hs.append(a_j * hs[-1] + run(b_ref, j))
                ps.append(a_j * ps[-1])
            q, e = ps[-1], hs[-1]
            for d in (1, 2, 4):
                q_s = jnp.where(row >= d, pltpu.roll(q, d, 0), 1.0)
                e_s = jnp.where(row >= d, pltpu.roll(e, d, 0), 0.0)
                e = q * e_s + e
                q = q * q_s
            after = q * c_in + e
            before = jnp.where(row == 0, c_in, pltpu.roll(after, 1, 0))
            c_in = after[SUBLANES - 1:SUBLANES, :]
            for j in range(SCAN_ROWS):
                h_ref[lc, pl.ds(blk * block + j, SUBLANES, stride=SCAN_ROWS), :] = hs[j] + ps[j] * before
        carries.append(c_in)
    h = jnp.concatenate([h_ref[lc] for lc in range(width // LANES)], axis=1)
    return (h * jax.nn.gelu(gate)).astype(bf16), jnp.concatenate(carries, axis=1)


def _proj_lru_kernel(tiles_per_seq, x_ref, w_ref, cw_ref, cb_ref, wr_ref, br_ref, wi_ref, bi_ref, lam_ref,
                     rest_ref, ya_ref, lx_ref, gt_ref, carry_ref, scan_ref):
    i = pl.program_id(0)
    slot = i % 2
    rslot = 1 - slot
    n_rc = PROJ_TM // LRU_T
    n_cg = D_MODEL // LRU_CW
    n_rest = REST_WIDTH // PROJ_TN
    assert n_rc * n_cg == n_rest + 1

    @pl.when(i == 0)
    def _():
        lx_ref[...] = jnp.zeros(lx_ref.shape, f32)
        gt_ref[...] = jnp.zeros(gt_ref.shape, f32)
        carry_ref[...] = jnp.zeros(carry_ref.shape, f32)

    prev_starts_seq = (i - 1) % tiles_per_seq == 0
    this_starts_seq = i % tiles_per_seq == 0
    xb = x_ref[...].astype(bf16)

    def lru_group(cg):
        lanes = slice(cg * LRU_CW, (cg + 1) * LRU_CW)
        xc = _conv(lx_ref[rslot, :, lanes], cw_ref[:, lanes], cb_ref[:, lanes])
        xcb = xc.astype(bf16)
        r_pre = jnp.dot(xcb, wr_ref[cg], preferred_element_type=f32)
        i_pre = jnp.dot(xcb, wi_ref[cg], preferred_element_type=f32)
        sp = jax.nn.softplus(-lam_ref[:, lanes])
        state = {"carry": jnp.where(prev_starts_seq, 0.0, carry_ref[:, lanes])}

        def finish(rc):
            rows = slice(rc * LRU_T, (rc + 1) * LRU_T)
            y, state["carry"] = _lru_rows(xc[rows], r_pre[rows], i_pre[rows], gt_ref[rslot, rows, lanes],
                                          state["carry"], br_ref[:, lanes], bi_ref[:, lanes], sp,
                                          scan_ref.at[rc % 2])
            ya_ref[rows, lanes] = y
            if rc == n_rc - 1:
                carry_ref[:, lanes] = state["carry"]

        return finish

    finish = lru_group(0)
    lx_ref[slot, SUBLANES:, :] = jnp.dot(xb, w_ref[:, 0:D_MODEL], preferred_element_type=f32)
    gt_ref[slot] = jnp.dot(xb, w_ref[:, D_MODEL:LRU_COLS], preferred_element_type=f32)
    tail = lx_ref[rslot, PROJ_TM:PROJ_TM + SUBLANES, :]
    lx_ref[slot, 0:SUBLANES, :] = jnp.where(this_starts_seq, 0.0, tail)

    for c in range(n_rc * n_cg):
        cg, rc = c // n_rc, c % n_rc
        if rc == 0 and cg > 0:
            finish = lru_group(cg)
        if c < n_rest:
            cols = slice(c * PROJ_TN, (c + 1) * PROJ_TN)
            acc = jnp.dot(xb, w_ref[:, LRU_COLS + c * PROJ_TN:LRU_COLS + (c + 1) * PROJ_TN],
                          preferred_element_type=f32)
            rest_ref[:, cols] = acc.astype(bf16)
        finish(rc)


def _proj_lru(x, w, cw, cb, wr, br, wi, bi, lam, seq, layer):
    m = x.shape[0]
    nt = m // PROJ_TM
    last = nt - 1
    vec = lambda rows: _resident((rows, D_MODEL), lambda i: (0, 0))
    mat = _resident((None, D_MODEL // LRU_CW, LRU_CW, LRU_CW), lambda i: (layer, 0, 0, 0))
    return pl.pallas_call(
        functools.partial(_proj_lru_kernel, seq // PROJ_TM),
        grid=(nt + 1,),
        in_specs=[pl.BlockSpec((PROJ_TM, D_MODEL), lambda i: (jnp.minimum(i, last), 0)),
                  _resident((None, D_MODEL, IN_WIDTH), lambda i: (layer, 0, 0)),
                  vec(CONV_WIDTH), vec(1), mat, vec(1), mat, vec(1), vec(1)],
        out_specs=[pl.BlockSpec((PROJ_TM, REST_WIDTH), lambda i: (jnp.minimum(i, last), 0)),
                   pl.BlockSpec((PROJ_TM, D_MODEL), lambda i: (jnp.maximum(i - 1, 0), 0))],
        out_shape=[jax.ShapeDtypeStruct((m, REST_WIDTH), bf16), jax.ShapeDtypeStruct((m, D_MODEL), bf16)],
        scratch_shapes=[pltpu.VMEM((2, PROJ_TM + SUBLANES, D_MODEL), f32),
                        pltpu.VMEM((2, PROJ_TM, D_MODEL), f32),
                        pltpu.VMEM((1, D_MODEL), f32),
                        pltpu.VMEM((2, 3, LRU_CW // LANES, LRU_T, LANES), f32)],
        compiler_params=_cparams(1),
        name="proj_lru",
    )(x, w, cw, cb, wr, br, wi, bi, lam)


def _band_bias(first_block):
    qi = lax.broadcasted_iota(jnp.int32, (WINDOW, 2 * WINDOW), 0)
    kj = lax.broadcasted_iota(jnp.int32, (WINDOW, 2 * WINDOW), 1)
    rel = qi + WINDOW - kj
    ok = jnp.logical_and(rel >= 0, rel <= WINDOW)
    if first_block:
        ok = jnp.logical_and(ok, kj >= WINDOW)
    return jnp.where(ok, 0.0, NEG_INF).astype(f32)


def _causal_bias():
    qi = lax.broadcasted_iota(jnp.int32, (WINDOW, WINDOW), 0)
    kj = lax.broadcasted_iota(jnp.int32, (WINDOW, WINDOW), 1)
    return jnp.where(qi >= kj, 0.0, NEG_INF).astype(f32)


def _nt_dot(a, b):
    return lax.dot_general(a, b, (((1,), (1,)), ((), ())), preferred_element_type=f32)


def _lane_lo():
    return lax.broadcasted_iota(jnp.int32, (1, LANES), 1) < HEAD_DIM


def _per_head(x, fill):
    lo = _lane_lo()
    f = jnp.full_like(x, fill)
    return jnp.where(lo, x, f), jnp.where(lo, f, x)


def _join_pair(a0, a1):
    lo = _lane_lo()
    return jnp.where(lo, a0, a1), pltpu.roll(jnp.where(lo, a1, a0), HEAD_DIM, 1)


def _run_pipeline(nblk, qk, softmax, pv, steps_per_trip):
    assert steps_per_trip % 2 == 0 and nblk % steps_per_trip == 0 and nblk >= 2 * steps_per_trip

    def step(t, slot):
        pv(t - 2, slot)
        qk(t, slot)
        softmax(t - 1, 1 - slot)

    qk(0, 0)
    qk(1, 1)
    softmax(0, 0)
    for t in range(2, steps_per_trip):
        step(t, t % 2)

    def trip(i, _):
        for u in range(steps_per_trip):
            step(steps_per_trip * i + u, u % 2)
        return 0

    lax.fori_loop(1, nblk // steps_per_trip, trip, 0)
    pv(nblk - 2, 0)
    softmax(nblk - 1, 1)
    pv(nblk - 1, 1)


def _block_rows(j):
    if isinstance(j, int):
        return pl.ds(max(j - 1, 0) * WINDOW, WINDOW), pl.ds(j * WINDOW, WINDOW)
    cur = pl.multiple_of(j * WINDOW, WINDOW)
    prev = pl.multiple_of(jnp.maximum(j - 1, 0) * WINDOW, WINDOW)
    return pl.ds(prev, WINDOW), pl.ds(cur, WINDOW)


def _swa_kernel(sink_ref, q_ref, k_ref, v_ref, o_ref, bias_ref, s_buf, p_buf, e_buf, kd_ref, vd_ref):
    seq = q_ref.shape[0]
    kvp = pl.program_id(1)
    lo = _lane_lo()

    @pl.when(jnp.logical_and(pl.program_id(0) == 0, kvp == 0))
    def _():
        bias_ref[0] = _band_bias(False)
        bias_ref[1] = _band_bias(True)

    lane_rolled_k = pltpu.roll(k_ref[...], HEAD_DIM, 1)
    lane_rolled_v = pltpu.roll(v_ref[...], HEAD_DIM, 1)
    for kvh in (0, 1):
        sel = lo if kvh == 0 else jnp.logical_not(lo)
        kd_ref[kvh] = jnp.where(sel, k_ref[...], lane_rolled_k)
        vd_ref[kvh] = jnp.where(sel, v_ref[...], lane_rolled_v)

    def kv_blocks(ref, kvh, j, fill):
        prev, cur = _block_rows(j)
        return _per_head(jnp.concatenate([ref[kvh, prev, :], ref[kvh, cur, :]], axis=0), fill)

    def qk(j, slot):
        _, cur = _block_rows(j)
        bias = bias_ref[jnp.where(j == 0, 1, 0)]
        for kvh in (0, 1):
            ks = kv_blocks(kd_ref, kvh, j, 0)
            for pair in (2 * kvh, 2 * kvh + 1):
                q2 = q_ref[cur, pair * LANES:(pair + 1) * LANES]
                for h in (0, 1):
                    s_buf[slot, 2 * pair + h] = _nt_dot(q2, ks[h]) + bias

    def softmax(j, slot):
        for pair in range(SWA_PAIRS):
            sink_terms = []
            for h in (0, 1):
                sink = sink_ref[kvp * 2 * SWA_PAIRS + 2 * pair + h] * LOG2E
                s = s_buf[slot, 2 * pair + h]
                m = jnp.maximum(jnp.max(s, axis=-1, keepdims=True), sink)
                p_buf[slot, 2 * pair + h] = jnp.exp2(s - m).astype(bf16)
                sink_terms.append(jnp.exp2(sink - m))
            e_buf[slot, pair] = jnp.where(lo, sink_terms[0], sink_terms[1])

    def pv(j, slot):
        _, cur = _block_rows(j)
        for kvh in (0, 1):
            vs = kv_blocks(vd_ref, kvh, j, 1)
            for pair in (2 * kvh, 2 * kvh + 1):
                accs = [jnp.dot(p_buf[slot, 2 * pair + h], vs[h], preferred_element_type=f32) for h in (0, 1)]
                num, den = _join_pair(*accs)
                den = den + e_buf[slot, pair]
                o_ref[cur, pair * LANES:(pair + 1) * LANES] = (num / den).astype(bf16)

    _run_pipeline(seq // WINDOW, qk, softmax, pv, SWA_STEPS)


def _swa(proj, sinks, batch, seq):
    m = batch * seq
    qw = SWA_PAIRS * LANES
    nh = 2 * SWA_PAIRS
    return pl.pallas_call(
        _swa_kernel,
        grid=(batch, 2),
        in_specs=[pl.BlockSpec(memory_space=pltpu.SMEM),
                  pl.BlockSpec((seq, qw), lambda b, p: (b, COL_QB // qw + p)),
                  pl.BlockSpec((seq, LANES), lambda b, p: (b, COL_KB // LANES + p)),
                  pl.BlockSpec((seq, LANES), lambda b, p: (b, COL_VB // LANES + p))],
        out_specs=pl.BlockSpec((seq, qw), lambda b, p: (b, p)),
        out_shape=jax.ShapeDtypeStruct((m, D_MODEL), bf16),
        scratch_shapes=[pltpu.VMEM((2, WINDOW, 2 * WINDOW), f32),
                        pltpu.VMEM((2, nh, WINDOW, 2 * WINDOW), f32),
                        pltpu.VMEM((2, nh, WINDOW, 2 * WINDOW), bf16),
                        pltpu.VMEM((2, SWA_PAIRS, WINDOW, LANES), f32),
                        pltpu.VMEM((2, seq, LANES), bf16), pltpu.VMEM((2, seq, LANES), bf16)],
        compiler_params=_cparams(2),
        name="swa",
    )(sinks, proj, proj, proj)


def _dil_kernel(q_ref, k_ref, v_ref, o_ref, qf, kf, vf, q4, k4, v4, q4b, k4b, v4b,
                num1, den1, m1, num2, den2, m2, num3, den3, m3, out_f, bias_ref, causal_ref, s_buf, p_buf):
    seq = q_ref.shape[0]
    lo = _lane_lo()
    sub = seq // DIL_2
    nb2 = sub // WINDOW

    @pl.when(jnp.logical_and(pl.program_id(0) == 0, pl.program_id(1) == 0))
    def _():
        bias_ref[0] = _band_bias(False)
        bias_ref[1] = _band_bias(True)
        causal_ref[...] = _causal_bias()

    for src, nat, four, four_b in ((q_ref, qf, q4, q4b), (k_ref, kf, k4, k4b), (v_ref, vf, v4, v4b)):
        nat[...] = src[...].astype(f32)
        for r in range(DIL_2):
            part = nat[pl.ds(r, sub, stride=DIL_2), :]
            four[r * sub:(r + 1) * sub, :] = part
            four_b[r * sub:(r + 1) * sub, :] = part.astype(bf16)

    def rows2(j):
        r, jb = j // nb2, j % nb2
        cur = pl.multiple_of(r * sub + jb * WINDOW, WINDOW)
        prev = pl.multiple_of(r * sub + jnp.maximum(jb - 1, 0) * WINDOW, WINDOW)
        return pl.ds(prev, WINDOW), pl.ds(cur, WINDOW), jb == 0

    def rows3(j):
        return pl.ds((j % DIL_2) * sub + j // DIL_2, WINDOW, stride=DIL_2)

    def gather_kv(ref, ref4, ref4b, j):
        prev1, cur1 = _block_rows(j)
        prev2, cur2, _ = rows2(j)
        return (jnp.concatenate([ref[prev1, :], ref[cur1, :]], axis=0),
                jnp.concatenate([ref4b[prev2, :], ref4b[cur2, :]], axis=0),
                ref4[rows3(j), :].astype(bf16))

    def qk(j, slot):
        _, cur1 = _block_rows(j)
        _, cur2, first2 = rows2(j)
        qs = (q_ref[cur1, :], q4b[cur2, :], q4[rows3(j), :].astype(bf16))
        ks = gather_kv(k_ref, k4, k4b, j)
        biases = (bias_ref[jnp.where(j == 0, 1, 0)], bias_ref[jnp.where(first2, 1, 0)], causal_ref[...])
        for c in range(DIL_PATTERNS):
            for h, k_h in enumerate(_per_head(ks[c], 0)):
                s = _nt_dot(qs[c], k_h) + biases[c]
                s_buf[slot, 2 * c + h, :, 0:s.shape[1]] = s

    def softmax(j, slot):
        _, cur1 = _block_rows(j)
        _, cur2, _ = rows2(j)
        for c, (m_ref, rows) in enumerate(((m1, cur1), (m2, cur2), (m3, rows3(j)))):
            nk = WINDOW if c == 2 else 2 * WINDOW
            ms = []
            for h in (0, 1):
                s = s_buf[slot, 2 * c + h, :, 0:nk]
                m = jnp.max(s, axis=-1, keepdims=True)
                p_buf[slot, 2 * c + h, :, 0:nk] = jnp.exp2(s - m).astype(bf16)
                ms.append(m)
            m_ref[rows, :] = jnp.where(lo, ms[0], ms[1])

    def pv(j, slot):
        _, cur1 = _block_rows(j)
        _, cur2, _ = rows2(j)
        vs = gather_kv(v_ref, v4, v4b, j)
        dests = ((num1, den1, cur1), (num2, den2, cur2), (num3, den3, rows3(j)))
        for c, (num_ref, den_ref, rows) in enumerate(dests):
            nk = WINDOW if c == 2 else 2 * WINDOW
            accs = [jnp.dot(p_buf[slot, 2 * c + h, :, 0:nk], v_h, preferred_element_type=f32)
                    for h, v_h in enumerate(_per_head(vs[c], 1))]
            num, den = _join_pair(*accs)
            num_ref[rows, :] = num
            den_ref[rows, :] = den

    _run_pipeline(seq // WINDOW, qk, softmax, pv, DIL_STEPS)

    ct = 256
    per_res = sub // ct

    def combine(c, _):
        r, part = c // per_res, c % per_res
        rows4 = pl.ds(pl.multiple_of(c * ct, ct), ct)
        rows_nat = pl.ds(r + DIL_2 * ct * part, ct, stride=DIL_2)
        ma, mb, mc = m1[rows_nat, :], m2[rows4, :], m3[rows4, :]
        mx = jnp.maximum(jnp.maximum(ma, mb), mc)
        wa, wb, wc = jnp.exp2(ma - mx), jnp.exp2(mb - mx), jnp.exp2(mc - mx)
        num = wa * num1[rows_nat, :] + wb * num2[rows4, :] + wc * num3[rows4, :]
        den = wa * den1[rows_nat, :] + wb * den2[rows4, :] + wc * den3[rows4, :]
        out_f[rows_nat, :] = num / den
        return 0

    lax.fori_loop(0, seq // ct, combine, 0)
    o_ref[...] = out_f[...].astype(bf16)


def _dil(proj, batch, seq):
    m = batch * seq
    npair = D_MODEL // LANES
    assert seq == DIL_3 * WINDOW, "pattern 3 is written for one 128-token block per subsequence"
    spec = lambda col: pl.BlockSpec((seq, LANES), lambda b, p: (b, col // LANES + p))
    nh = 2 * DIL_PATTERNS
    tok_f = pltpu.VMEM((seq, LANES), f32)
    tok_b = pltpu.VMEM((seq, LANES), bf16)
    return pl.pallas_call(
        _dil_kernel,
        grid=(batch, npair),
        in_specs=[spec(COL_QC), spec(COL_KC), spec(COL_VC)],
        out_specs=pl.BlockSpec((seq, LANES), lambda b, p: (b, p)),
        out_shape=jax.ShapeDtypeStruct((m, D_MODEL), bf16),
        scratch_shapes=[tok_f] * 6 + [tok_b] * 3 + [tok_f] * 10
        + [pltpu.VMEM((2, WINDOW, 2 * WINDOW), f32), pltpu.VMEM((WINDOW, WINDOW), f32),
           pltpu.VMEM((2, nh, WINDOW, 2 * WINDOW), f32), pltpu.VMEM((2, nh, WINDOW, 2 * WINDOW), bf16)],
        compiler_params=_cparams(2),
        name="dil",
    )(proj, proj, proj)


def _layer_norm(z, g, b):
    mu = jnp.mean(z, axis=-1, keepdims=True)
    zc = z - mu
    var = jnp.mean(zc * zc, axis=-1, keepdims=True)
    return zc * lax.rsqrt(var + LN_EPS) * g + b


def _merge_kernel(x_ref, ya_ref, yb_ref, yc_ref, gates_ref, wb_ref, wo_ref, g_ref, b_ref, o_ref):
    for t in range(ROW_TM // ROW_SUB):
        rows = slice(t * ROW_SUB, (t + 1) * ROW_SUB)
        merged = None
        for n, y_ref in enumerate((ya_ref, yb_ref, yc_ref)):
            branch = jnp.dot(y_ref[rows, :], wb_ref[n], preferred_element_type=f32)
            gate = jax.nn.sigmoid(gates_ref[rows, n * D_MODEL:(n + 1) * D_MODEL].astype(f32))
            term = gate * branch
            merged = term if merged is None else merged + term
        mix = jnp.dot(merged.astype(bf16), wo_ref[...], preferred_element_type=f32)
        o_ref[rows, :] = _layer_norm(ALPHA * x_ref[rows, :] + mix, g_ref[...], b_ref[...])


def _merge(x, ya, yb, yc, proj, wb, wo, g, b, layer):
    m = x.shape[0]
    row = lambda width, col_block: pl.BlockSpec((ROW_TM, width), lambda i: (i, col_block))
    gw = N_BRANCHES * D_MODEL
    return pl.pallas_call(
        _merge_kernel,
        grid=(m // ROW_TM,),
        in_specs=[row(D_MODEL, 0), row(D_MODEL, 0), row(D_MODEL, 0), row(D_MODEL, 0), row(gw, COL_GATES // gw),
                  _resident((None, N_BRANCHES, D_MODEL, D_MODEL), lambda i: (layer, 0, 0, 0)),
                  _resident((None, D_MODEL, D_MODEL), lambda i: (layer, 0, 0)),
                  _resident((1, D_MODEL), lambda i: (0, 0)),
                  _resident((1, D_MODEL), lambda i: (0, 0))],
        out_specs=row(D_MODEL, 0),
        out_shape=jax.ShapeDtypeStruct((m, D_MODEL), f32),
        compiler_params=_cparams(1),
        name="merge",
    )(x, ya, yb, yc, proj, wb, wo, g, b)


def _ffn_kernel(x_ref, win_ref, wout_ref, g_ref, b_ref, o_ref):
    for t in range(ROW_TM // ROW_SUB):
        rows = slice(t * ROW_SUB, (t + 1) * ROW_SUB)
        x = x_ref[rows, :]
        xb = x.astype(bf16)
        acc = None
        for c in range(FF_HIDDEN // FFN_CH):
            h1 = jnp.dot(xb, win_ref[:, c * FFN_CH:(c + 1) * FFN_CH], preferred_element_type=f32)
            h3 = jnp.dot(xb, win_ref[:, FF_HIDDEN + c * FFN_CH:FF_HIDDEN + (c + 1) * FFN_CH],
                         preferred_element_type=f32)
            act = (jax.nn.silu(h1) * h3).astype(bf16)
            part = jnp.dot(act, wout_ref[c * FFN_CH:(c + 1) * FFN_CH, :], preferred_element_type=f32)
            acc = part if acc is None else acc + part
        o_ref[rows, :] = _layer_norm(ALPHA * x + acc, g_ref[...], b_ref[...])


def _ffn(x, win, wout, g, b, layer):
    m = x.shape[0]
    return pl.pallas_call(
        _ffn_kernel,
        grid=(m // ROW_TM,),
        in_specs=[pl.BlockSpec((ROW_TM, D_MODEL), lambda i: (i, 0)),
                  _resident((None, D_MODEL, 2 * FF_HIDDEN), lambda i: (layer, 0, 0)),
                  _resident((None, FF_HIDDEN, D_MODEL), lambda i: (layer, 0, 0)),
                  _resident((1, D_MODEL), lambda i: (0, 0)),
                  _resident((1, D_MODEL), lambda i: (0, 0))],
        out_specs=pl.BlockSpec((ROW_TM, D_MODEL), lambda i: (i, 0)),
        out_shape=jax.ShapeDtypeStruct((m, D_MODEL), f32),
        compiler_params=_cparams(1),
        name="ffn",
    )(x, win, wout, g, b)


def _prep_w_in(w_in):
    scale = HEAD_DIM ** -0.5 * LOG2E
    lru_x, lru_g, qb, kb, vb, qc, kc, vc, gates = jnp.split(
        w_in, [1024, 2048, 3072, 3328, 3584, 4608, 5632, 6656], axis=-1)
    pieces = [lru_x, lru_g, gates, qb * scale, qc * scale, kc, vc, kb, vb]
    return jnp.concatenate([p.astype(bf16) for p in pieces], axis=-1)


def _block_diag(w):
    per = LRU_CW // HEAD_DIM
    w = w.reshape(DEPTH, LRU_BLOCKS // per, per, HEAD_DIM, HEAD_DIM)
    eye = jnp.eye(per, dtype=w.dtype)
    bd = jnp.einsum("lgaij,ab->lgaibj", w, eye)
    return bd.reshape(DEPTH, LRU_BLOCKS // per, LRU_CW, LRU_CW).astype(bf16)


def kernel(x, w_in, conv_w, conv_b, w_rg, b_rg, w_ig, b_ig, lru_lambda, sinks, w_branch, w_out,
           ln1_g, ln1_b, w_ffn_in, w_ffn_out, ln2_g, ln2_b):
    batch, seq, d = x.shape
    assert d == D_MODEL and w_in.shape == (DEPTH, D_MODEL, IN_WIDTH)
    w_in_b = _prep_w_in(w_in)
    wr_b = _block_diag(w_rg)
    wi_b = _block_diag(w_ig)
    wb_b = w_branch.astype(bf16)
    wo_b = w_out.astype(bf16)
    wfi_b = w_ffn_in.astype(bf16)
    wfo_b = w_ffn_out.astype(bf16)
    row = lambda p, l: p[l].reshape(1, D_MODEL)

    h = x.reshape(batch * seq, D_MODEL)
    for l in range(DEPTH):
        proj, ya = _proj_lru(h, w_in_b, conv_w[l], row(conv_b, l), wr_b, row(b_rg, l), wi_b,
                             row(b_ig, l), row(lru_lambda, l), seq, l)
        yb = _swa(proj, sinks[l], batch, seq)
        yc = _dil(proj, batch, seq)
        h = _merge(h, ya, yb, yc, proj, wb_b, wo_b, row(ln1_g, l), row(ln1_b, l), l)
        h = _ffn(h, wfi_b, wfo_b, row(ln2_g, l), row(ln2_b, l), l)
    return h.reshape(batch, seq, D_MODEL)
```
